```python
import math
import jax, jax.numpy as jnp
from jax import lax
import numpy as np

D_MODEL = 1024
BATCH = 16
SEQ = 2048
DEPTH = 2

D_FF = 4 * D_MODEL
EPS = 1e-6
NEG_BIG = -1e30
M_HEADS = 4
M_DIM = D_MODEL // 16
M_WIDTH = M_HEADS * M_DIM
M_CONV = 4
M_CHUNK = 128
A_HEADS = 4
A_DIM = D_MODEL // 16
A_WIDTH = A_HEADS * 2 * A_DIM
ROT_DIM = A_DIM // 4
ROPE_THETA = 500000.0
Q_BLOCK = 128
H_HEADS = 4
H_DK = D_MODEL // 16
H_DV = D_MODEL // 16
H_WIDTH = H_HEADS * H_DV
H_CHUNK = 64
D_MIX = M_WIDTH + A_WIDTH + H_WIDTH
M_COLS = 4 * M_WIDTH + 2 * M_HEADS
A_COLS = 3 * A_WIDTH
H_COLS = 4 * H_WIDTH
D_IN = M_COLS + A_COLS + H_COLS

kernel_name = 'hymba_style_mlstm_diffattn_hgrn2'


def rmsnorm(x, g):
    xf = x.astype(jnp.float32)
    y = xf * lax.rsqrt(jnp.mean(xf * xf, axis=-1, keepdims=True) + EPS)
    return (y * g.astype(jnp.float32)).astype(x.dtype)


def causal_dwconv(x, w):
    K = w.shape[0]
    T = x.shape[1]
    xp = jnp.pad(x, ((0, 0), (K - 1, 0), (0, 0)))
    out = xp[:, 0:T] * w[0]
    for j in range(1, K):
        out = out + xp[:, j:j + T] * w[j]
    return out


def partial_rope(x, pos):
    half = ROT_DIM // 2
    inv = ROPE_THETA ** (-jnp.arange(half, dtype=jnp.float32) / half)
    ang = pos.astype(jnp.float32)[:, None] * inv[None, :]
    cos = jnp.cos(ang)[:, None, :]
    sin = jnp.sin(ang)[:, None, :]
    x1 = x[..., :half]
    x2 = x[..., half:ROT_DIM]
    r1 = x1 * cos - x2 * sin
    r2 = x2 * cos + x1 * sin
    return jnp.concatenate([r1, r2, x[..., ROT_DIM:]], axis=-1)


def to_chunks(a, L):
    B, H, T = a.shape[:3]
    a = a.reshape((B, H, T // L, L) + a.shape[3:])
    return jnp.moveaxis(a, 2, 0)


def from_chunks(a):
    a = jnp.moveaxis(a, 0, 2)
    B, H, NC, L = a.shape[:4]
    return a.reshape((B, H, NC * L) + a.shape[4:])


def mlstm_chunkwise(q, k, v, i_pre, f_pre):
    B, H, T, d = q.shape
    L = M_CHUNK
    k = k * (d ** -0.5)
    logf = jax.nn.log_sigmoid(f_pre)
    xs = (to_chunks(q, L), to_chunks(k, L), to_chunks(v, L), to_chunks(i_pre, L), to_chunks(logf, L))
    causal = jnp.tril(jnp.ones((L, L), dtype=bool))

    def step(carry, xs_c):
        C, n, m = carry
        qb, kb, vb, ib, fb = xs_c
        b = jnp.cumsum(fb, axis=-1)
        D = b[..., :, None] - b[..., None, :] + ib[..., None, :]
        D = jnp.where(causal, D, NEG_BIG)
        inter = b + m[..., None]
        m_t = jnp.maximum(jnp.max(D, axis=-1), inter)
        S = jnp.einsum('bhtd,bhsd->bhts', qb, kb) * jnp.exp(D - m_t[..., None])
        scale = jnp.exp(inter - m_t)
        num = jnp.einsum('bhts,bhsv->bhtv', S, vb) + scale[..., None] * jnp.einsum('bhvk,bhtk->bhtv', C, qb)
        den = jnp.sum(S, axis=-1) + scale * jnp.einsum('bhk,bhtk->bht', n, qb)
        h = num / jnp.maximum(jnp.abs(den), jnp.exp(-m_t))[..., None]
        bL = b[..., -1]
        g = bL[..., None] - b + ib
        m_new = jnp.maximum(bL + m, jnp.max(g, axis=-1))
        w_s = jnp.exp(g - m_new[..., None])
        decay = jnp.exp(bL + m - m_new)
        C_new = decay[..., None, None] * C + jnp.einsum('bhs,bhsv,bhsk->bhvk', w_s, vb, kb)
        n_new = decay[..., None] * n + jnp.einsum('bhs,bhsk->bhk', w_s, kb)
        return (C_new, n_new, m_new), h

    init = (jnp.zeros((B, H, d, d), jnp.float32), jnp.zeros((B, H, d), jnp.float32), jnp.zeros((B, H), jnp.float32))
    _, h = lax.scan(step, init, xs)
    return from_chunks(h)


def mlstm_mixer(q_pre, k_pre, v, o_pre, i_pre, f_pre, conv_w, gate_b, norm_g):
    B, T, _ = q_pre.shape
    f32 = jnp.float32
    qk = jax.nn.silu(causal_dwconv(jnp.concatenate([q_pre, k_pre], axis=-1).astype(f32), conv_w.astype(f32)))
    q, k = qk[..., :M_WIDTH], qk[..., M_WIDTH:]
    heads = lambda a: a.reshape(B, T, M_HEADS, M_DIM).transpose(0, 2, 1, 3)
    gb = gate_b.astype(f32)
    i_g = (i_pre.astype(f32) + gb[0]).transpose(0, 2, 1)
    f_g = (f_pre.astype(f32) + gb[1]).transpose(0, 2, 1)
    h = mlstm_chunkwise(heads(q), heads(k), heads(v.astype(f32)), i_g, f_g)
    h = rmsnorm(h.transpose(0, 2, 1, 3), norm_g)
    h = jax.nn.sigmoid(o_pre.astype(f32)).reshape(B, T, M_HEADS, M_DIM) * h
    return h.reshape(B, T, M_WIDTH)


def diff_attention(q, k, v, qn_g, kn_g, lam_vec, subln_g, lambda_init):
    B, T, _ = q.shape
    f32 = jnp.float32
    pos = jnp.arange(T)
    q = rmsnorm(q.astype(f32).reshape(B, T, A_HEADS, 2, A_DIM), qn_g)
    k = rmsnorm(k.astype(f32).reshape(B, T, A_HEADS, 2, A_DIM), kn_g)
    q = partial_rope(q.reshape(B, T, A_HEADS * 2, A_DIM), pos).reshape(B, T, A_HEADS, 2, A_DIM)
    k = partial_rope(k.reshape(B, T, A_HEADS * 2, A_DIM), pos).reshape(B, T, A_HEADS, 2, A_DIM)
    q = q * (A_DIM ** -0.5)
    v = v.astype(f32).reshape(B, T, A_HEADS, 2 * A_DIM).transpose(0, 2, 1, 3)
    kt = k.transpose(0, 2, 3, 1, 4)
    lv = lam_vec.astype(f32)
    lam = jnp.exp(jnp.sum(lv[0] * lv[1])) - jnp.exp(jnp.sum(lv[2] * lv[3])) + lambda_init
    NB = T // Q_BLOCK
    qb = q.reshape(B, NB, Q_BLOCK, A_HEADS, 2, A_DIM).transpose(1, 0, 3, 4, 2, 5)
    kpos = jnp.arange(T)

    def block(args):
        qi, bi = args
        s = jnp.einsum('bhcqd,bhckd->bhcqk', qi, kt)
        qpos = bi * Q_BLOCK + jnp.arange(Q_BLOCK)
        mask = qpos[:, None] >= kpos[None, :]
        p = jax.nn.softmax(jnp.where(mask, s, NEG_BIG), axis=-1)
        a = p[:, :, 0] - lam * p[:, :, 1]
        return jnp.einsum('bhqk,bhkv->bhqv', a, v)

    o = lax.map(block, (qb, jnp.arange(NB)))
    o = o.transpose(1, 0, 3, 2, 4).reshape(B, T, A_HEADS, 2 * A_DIM)
    o = rmsnorm(o, subln_g) * (1.0 - lambda_init)
    return o.reshape(B, T, A_WIDTH)


def hgrn2_chunkwise(q, logf, k, v):
    B, H, T, dk = q.shape
    dv = v.shape[-1]
    L = H_CHUNK
    xs = (to_chunks(q, L), to_chunks(logf, L), to_chunks(k, L), to_chunks(v, L))
    causal = jnp.tril(jnp.ones((L, L), dtype=bool))[..., None]

    def step(S, xs_c):
        qb, gb, kb, vb = xs_c
        Bc = jnp.cumsum(gb, axis=-2)
        diff = Bc[..., :, None, :] - Bc[..., None, :, :]
        decay = jnp.exp(jnp.where(causal, diff, NEG_BIG))
        A = jnp.einsum('bhtk,bhsk,bhtsk->bhts', qb, kb, decay)
        o = jnp.einsum('bhts,bhsv->bhtv', A, vb) + jnp.einsum('bhtk,bhkv->bhtv', qb * jnp.exp(Bc), S)
        BL = Bc[..., -1:, :]
        S_new = jnp.exp(BL[..., 0, :])[..., None] * S + jnp.einsum('bhsk,bhsv->bhkv', kb * jnp.exp(BL - Bc), vb)
        return S_new, o

    _, o = lax.scan(step, jnp.zeros((B, H, dk, dv), jnp.float32), xs)
    return from_chunks(o)


def hgrn2_mixer(q_pre, f_pre, i_pre, g_pre, lb, norm_g):
    B, T, _ = q_pre.shape
    f32 = jnp.float32
    heads = lambda a, dd: a.astype(f32).reshape(B, T, H_HEADS, dd).transpose(0, 2, 1, 3)
    q = jax.nn.silu(heads(q_pre, H_DK))
    fp = heads(f_pre, H_DK)
    lbh = lb.reshape(H_HEADS, 1, H_DK)
    f = lbh + (1.0 - lbh) * jax.nn.sigmoid(fp)
    logf = jnp.log(f)
    k = (1.0 - lbh) * jax.nn.sigmoid(-fp)
    v = heads(i_pre, H_DV)
    o = hgrn2_chunkwise(q, logf, k, v).transpose(0, 2, 1, 3)
    o = rmsnorm(o, norm_g) * jax.nn.silu(g_pre.astype(f32)).reshape(B, T, H_HEADS, H_DV)
    return o.reshape(B, T, H_WIDTH)


def setup_inputs(seed: int = 0) -> dict:
    key = jax.random.key(seed)
    ks = jax.random.split(key, 20)
    f32 = jnp.float32
    nrm = lambda k, shp, s: jax.random.normal(k, shp, f32) * s
    f_bias = jnp.broadcast_to(jnp.linspace(3.0, 6.0, M_HEADS, dtype=f32), (DEPTH, M_HEADS))
    mlstm_gate_b = jnp.stack([nrm(ks[4], (DEPTH, M_HEADS), 0.1), f_bias + nrm(ks[5], (DEPTH, M_HEADS), 0.1)], axis=1)
    return {
        'x': jax.random.normal(ks[0], (BATCH, SEQ, D_MODEL), f32),
        'norm1_g': 1.0 + nrm(ks[1], (DEPTH, D_MODEL), 0.02),
        'w_in': nrm(ks[2], (DEPTH, D_MODEL, D_IN), D_MODEL ** -0.5),
        'mlstm_conv_w': nrm(ks[3], (DEPTH, M_CONV, 2 * M_WIDTH), M_CONV ** -0.5),
        'mlstm_gate_b': mlstm_gate_b,
        'mlstm_norm_g': 1.0 + nrm(ks[6], (DEPTH, M_DIM), 0.02),
        'diff_q_norm_g': 1.0 + nrm(ks[7], (DEPTH, A_DIM), 0.02),
        'diff_k_norm_g': 1.0 + nrm(ks[8], (DEPTH, A_DIM), 0.02),
        'diff_lambda': nrm(ks[9], (DEPTH, 4, A_DIM), 0.1),
        'diff_subln_g': 1.0 + nrm(ks[10], (DEPTH, 2 * A_DIM), 0.02),
        'hgrn_lb_param': nrm(ks[11], (DEPTH, H_HEADS * H_DK), 1.0),
        'hgrn_norm_g': 1.0 + nrm(ks[12], (DEPTH, H_DV), 0.02),
        'w_out': nrm(ks[13], (DEPTH, D_MIX, D_MODEL), D_MIX ** -0.5),
        'norm2_g': 1.0 + nrm(ks[14], (DEPTH, D_MODEL), 0.02),
        'w_up': nrm(ks[15], (DEPTH, D_MODEL, D_FF), D_MODEL ** -0.5),
        'w_down': nrm(ks[16], (DEPTH, D_FF, D_MODEL), D_FF ** -0.5),
    }


def reference(x, norm1_g, w_in, mlstm_conv_w, mlstm_gate_b, mlstm_norm_g, diff_q_norm_g, diff_k_norm_g,
              diff_lambda, diff_subln_g, hgrn_lb_param, hgrn_norm_g, w_out, norm2_g, w_up, w_down):
    sm = jax.nn.softmax(hgrn_lb_param.astype(jnp.float32), axis=0)
    lower_bounds = jnp.cumsum(sm, axis=0) - sm[0]
    split_at = np.cumsum([M_WIDTH, M_WIDTH, M_WIDTH, M_WIDTH, M_HEADS, M_HEADS,
                          A_WIDTH, A_WIDTH, A_WIDTH,
                          H_WIDTH, H_WIDTH, H_WIDTH])
    for l in range(DEPTH):
        lambda_init = 0.8 - 0.6 * math.exp(-0.3 * l)
        h = rmsnorm(x, norm1_g[l])
        u = jnp.einsum('btd,de->bte', h, w_in[l])
        (mq, mk, mv, mo, mi, mf, aq, ak, av, hq, hf, hi, hg) = jnp.split(u, split_at, axis=-1)
        y_m = mlstm_mixer(mq, mk, mv, mo, mi, mf, mlstm_conv_w[l], mlstm_gate_b[l], mlstm_norm_g[l])
        y_a = diff_attention(aq, ak, av, diff_q_norm_g[l], diff_k_norm_g[l], diff_lambda[l], diff_subln_g[l], lambda_init)
        y_h = hgrn2_mixer(hq, hf, hi, hg, lower_bounds[l], hgrn_norm_g[l])
        mix = jnp.concatenate([y_m, y_a, y_h], axis=-1).astype(x.dtype)
        x = x + jnp.einsum('bte,ed->btd', mix, w_out[l])
        h2 = rmsnorm(x, norm2_g[l])
        a = jnp.square(jax.nn.relu(jnp.einsum('btd,df->btf', h2, w_up[l])))
        x = x + jnp.einsum('btf,fd->btd', a, w_down[l])
    return x
```

```python
import functools
import math

import jax
import jax.numpy as jnp
from jax import lax
from jax.experimental import pallas as pl
from jax.experimental.pallas import tpu as pltpu

F32 = jnp.float32
BF16 = jnp.bfloat16

D_MODEL = 1024
DEPTH = 2
D_FF = 4 * D_MODEL
EPS = 1e-6
NEG_BIG = -1e30
HEAD_DIM = 64
M_WIDTH = 256
M_CONV = 4
A_WIDTH = 512
A_HEADS = 4
ROT_DIM = 16
ROPE_THETA = 500000.0
H_WIDTH = 256
LANES = 128

VMEM_LIMIT = 56 * 1024 * 1024

ROW_TILE = 512
COL_CHUNK = 512
M_CHUNK = 128
H_BLOCK = 256
H_SUB = 16
Q_TILE = 256
K_TILE = 256
PREP_TILE = 512


def _nt_dot(a, b):
    return lax.dot_general(a, b, (((1,), (1,)), ((), ())), preferred_element_type=F32)


def _dot(a, b):
    return jnp.dot(a, b, preferred_element_type=F32)


def _block_ones(n, group):
    r = lax.broadcasted_iota(jnp.int32, (n, n), 0) // group
    c = lax.broadcasted_iota(jnp.int32, (n, n), 1) // group
    return jnp.where(r == c, 1.0, 0.0).astype(BF16)


def _group_mean(xsq, ones_bd, group):
    hi = xsq.astype(BF16)
    lo = (xsq - hi.astype(F32)).astype(BF16)
    return (_dot(hi, ones_bd) + _dot(lo, ones_bd)) * (1.0 / group)


def _log_sigmoid(x):
    return jnp.minimum(x, 0.0) - jnp.log1p(jnp.exp(-jnp.abs(x)))


def _inproj_kernel(x_ref, g_ref, wm_ref, wa_ref, wh_ref, wg_ref, wgt_ref, bg_ref, bgt_ref,
                   um_ref, ua_ref, uh_ref, gc_ref, gt_ref):
    x = x_ref[...]
    ms = jnp.mean(x * x, axis=-1, keepdims=True)
    h = (x * lax.rsqrt(ms + EPS) * g_ref[...]).astype(BF16)
    for w_ref, o_ref in ((wm_ref, um_ref), (wa_ref, ua_ref), (wh_ref, uh_ref)):
        for c in range(w_ref.shape[1] // COL_CHUNK):
            sl = slice(c * COL_CHUNK, (c + 1) * COL_CHUNK)
            o_ref[:, sl] = _dot(h, w_ref[:, sl])
    gc_ref[...] = _dot(h, wg_ref[...]) + bg_ref[...]
    gt_ref[...] = _nt_dot(wgt_ref[...], h) + bgt_ref[:, 0:1]


def _inproj(x2, g, wm, wa, wh, wg, wgt, bg, bgt):
    n = x2.shape[0]
    const = lambda shape: pl.BlockSpec(shape, lambda i: (0, 0))
    rows = lambda width: pl.BlockSpec((ROW_TILE, width), lambda i: (i, 0))
    return pl.pallas_call(
        _inproj_kernel,
        grid=(n // ROW_TILE,),
        in_specs=[rows(D_MODEL), const((1, D_MODEL)), const(wm.shape), const(wa.shape), const(wh.shape),
                  const(wg.shape), const(wgt.shape), const(bg.shape), const(bgt.shape)],
        out_specs=[rows(wm.shape[1]), rows(wa.shape[1]), rows(wh.shape[1]), rows(LANES),
                   pl.BlockSpec((16, ROW_TILE), lambda i: (0, i))],
        out_shape=[jax.ShapeDtypeStruct((n, wm.shape[1]), F32),
                   jax.ShapeDtypeStruct((n, wa.shape[1]), F32),
                   jax.ShapeDtypeStruct((n, wh.shape[1]), F32),
                   jax.ShapeDtypeStruct((n, LANES), F32),
                   jax.ShapeDtypeStruct((16, n), F32)],
        compiler_params=pltpu.CompilerParams(dimension_semantics=("arbitrary",), vmem_limit_bytes=VMEM_LIMIT),
        name="inproj",
    )(x2, g, wm, wa, wh, wg, wgt, bg, bgt)


def _post_kernel(x_ref, ym_ref, ya_ref, yh_ref, wo_ref, g2_ref, wu_ref, wd_ref, o_ref):
    x1 = x_ref[...]
    x1 = x1 + _dot(ym_ref[...], wo_ref[0:M_WIDTH, :])
    x1 = x1 + _dot(ya_ref[...], wo_ref[M_WIDTH:M_WIDTH + A_WIDTH, :])
    x1 = x1 + _dot(yh_ref[...], wo_ref[M_WIDTH + A_WIDTH:, :])
    ms = jnp.mean(x1 * x1, axis=-1, keepdims=True)
    h2 = (x1 * lax.rsqrt(ms + EPS) * g2_ref[...]).astype(BF16)
    o_ref[...] = x1
    for c in range(D_FF // COL_CHUNK):
        sl = slice(c * COL_CHUNK, (c + 1) * COL_CHUNK)
        a = jnp.maximum(_dot(h2, wu_ref[:, sl]), 0.0)
        o_ref[...] += _dot((a * a).astype(BF16), wd_ref[sl, :])


def _post(x2, ym, ya, yh, wo, g2, wu, wd):
    n = x2.shape[0]
    const = lambda shape: pl.BlockSpec(shape, lambda i: (0, 0), pipeline_mode=pl.Buffered(1))
    rows = lambda width: pl.BlockSpec((ROW_TILE, width), lambda i: (i, 0))
    return pl.pallas_call(
        _post_kernel,
        grid=(n // ROW_TILE,),
        in_specs=[rows(D_MODEL), rows(M_WIDTH), rows(A_WIDTH), rows(H_WIDTH),
                  const(wo.shape), const((1, D_MODEL)), const(wu.shape), const(wd.shape)],
        out_specs=rows(D_MODEL),
        out_shape=jax.ShapeDtypeStruct((n, D_MODEL), F32),
        compiler_params=pltpu.CompilerParams(dimension_semantics=("arbitrary",), vmem_limit_bytes=VMEM_LIMIT),
        name="post",
    )(x2, ym, ya, yh, wo, g2, wu, wd)


def _mlstm_kernel(um_ref, gc_ref, gt_ref, cw_ref, ng_ref, y_ref, xpad_ref, gn_ref, m_ref):
    L = M_CHUNK
    c = pl.program_id(1)

    @pl.when(c == 0)
    def _():
        xpad_ref[0:8, :] = jnp.zeros((8, 2 * M_WIDTH), F32)
        gn_ref[...] = jnp.zeros(gn_ref.shape, F32)
        m_ref[...] = jnp.zeros(m_ref.shape, F32)

    xin = um_ref[0, :, 0:2 * M_WIDTH]
    xpad_ref[8:8 + L, :] = xin
    conv = xpad_ref[5:5 + L, :] * cw_ref[0:1, :]
    conv = conv + xpad_ref[6:6 + L, :] * cw_ref[1:2, :]
    conv = conv + xpad_ref[7:7 + L, :] * cw_ref[2:3, :]
    conv = conv + xin * cw_ref[3:4, :]
    xpad_ref[0:8, :] = xin[L - 8:L, :]
    qk = conv * jax.nn.sigmoid(conv)
    q_all = qk[:, 0:M_WIDTH]
    k_all = qk[:, M_WIDTH:] * (HEAD_DIM ** -0.5)
    v_all = um_ref[0, :, 2 * M_WIDTH:3 * M_WIDTH]
    o_all = um_ref[0, :, 3 * M_WIDTH:4 * M_WIDTH]

    gcol = gc_ref[0]
    grow = gt_ref[...]
    logf_col = _log_sigmoid(gcol)
    logf_row = _log_sigmoid(grow)

    ti = lax.broadcasted_iota(jnp.int32, (L, L), 0)
    si = lax.broadcasted_iota(jnp.int32, (L, L), 1)
    causal = si <= ti
    lane = lax.broadcasted_iota(jnp.int32, (L, LANES), 1)
    left = lane < HEAD_DIM
    ones_bd = _block_ones(LANES, HEAD_DIM)
    gr = lax.broadcasted_iota(jnp.int32, (LANES, 2 * LANES), 0)
    gc_i = lax.broadcasted_iota(jnp.int32, (LANES, 2 * LANES), 1)
    g_rowleft = gr < HEAD_DIM
    g_block = g_rowleft == ((gc_i % LANES) < HEAD_DIM)

    for p in range(2):
        sl = slice(p * LANES, (p + 1) * LANES)
        q_pair = q_all[:, sl]
        k_pair = k_all[:, sl]
        v1 = jnp.concatenate([v_all[:, sl], jnp.ones((L, LANES), F32)], axis=1).astype(BF16)
        k_bf = k_pair.astype(BF16)
        intra, scale, einv, wcol, decay = [], [], [], [], []
        for hh in range(2):
            h = 2 * p + hh
            m_prev = m_ref[h:h + 1, 0:1]
            i_row = grow[h:h + 1, :]
            i_col = gcol[:, h:h + 1]
            lf_row = logf_row[4 + h:5 + h, :]
            lf_col = logf_col[:, 4 + h:5 + h]
            b_col = jnp.sum(jnp.where(causal, lf_row, 0.0), axis=1, keepdims=True)
            b_row = jnp.sum(jnp.where(ti <= si, lf_col, 0.0), axis=0, keepdims=True)
            a_row = i_row - b_row
            a_col = i_col - b_col
            dm = jnp.where(causal, a_row, NEG_BIG)
            m_col = jnp.maximum(jnp.max(dm, axis=1, keepdims=True), m_prev)
            e = jnp.exp(dm - m_col)
            qm = jnp.where(left if hh == 0 else jnp.logical_not(left), q_pair, 0.0).astype(BF16)
            s = _nt_dot(qm, k_bf) * e
            intra.append(_dot(s.astype(BF16), v1))
            m_last = m_col[L - 1:L, :]
            scale.append(jnp.exp(m_prev - m_col))
            einv.append(jnp.exp(-(b_col + m_col)))
            wcol.append(jnp.exp(a_col - m_last))
            decay.append(jnp.exp(m_prev - m_last))
            m_ref[h:h + 1, :] = jnp.broadcast_to(b_col[L - 1:L, :] + m_last, (1, LANES))

        gn = gn_ref[p]
        inter = _dot(q_pair.astype(BF16), gn.astype(BF16))
        sc = jnp.where(left, scale[0], scale[1])
        num = jnp.where(left, intra[0][:, :LANES], intra[1][:, :LANES]) + sc * inter[:, :LANES]
        den = jnp.where(left, intra[0][:, LANES:], intra[1][:, LANES:]) + sc * inter[:, LANES:]
        hv = num / jnp.maximum(jnp.abs(den), jnp.where(left, einv[0], einv[1]))

        kw = k_pair * jnp.where(left, wcol[0], wcol[1])
        upd = _dot(kw.T.astype(BF16), v1)
        gn_ref[p] = jnp.where(g_rowleft, decay[0], decay[1]) * gn + jnp.where(g_block, upd, 0.0)

        ms = _group_mean(hv * hv, ones_bd, HEAD_DIM)
        y = jax.nn.sigmoid(o_all[:, sl]) * (hv * lax.rsqrt(ms + EPS) * ng_ref[...])
        y_ref[0, :, sl] = y.astype(y_ref.dtype)


def _mlstm(um, gc, gt, conv_w, ng):
    b, t, _ = um.shape
    nc = t // M_CHUNK
    return pl.pallas_call(
        _mlstm_kernel,
        grid=(b, nc),
        in_specs=[pl.BlockSpec((1, M_CHUNK, 4 * M_WIDTH), lambda i, c: (i, c, 0)),
                  pl.BlockSpec((1, M_CHUNK, LANES), lambda i, c: (i, c, 0)),
                  pl.BlockSpec((16, M_CHUNK), lambda i, c: (0, i * nc + c)),
                  pl.BlockSpec((M_CONV, 2 * M_WIDTH), lambda i, c: (0, 0)),
                  pl.BlockSpec((1, LANES), lambda i, c: (0, 0))],
        out_specs=pl.BlockSpec((1, M_CHUNK, M_WIDTH), lambda i, c: (i, c, 0)),
        out_shape=jax.ShapeDtypeStruct((b, t, M_WIDTH), BF16),
        scratch_shapes=[pltpu.VMEM((M_CHUNK + 8, 2 * M_WIDTH), F32),
                        pltpu.VMEM((2, LANES, 2 * LANES), F32),
                        pltpu.VMEM((8, LANES), F32)],
        compiler_params=pltpu.CompilerParams(dimension_semantics=("arbitrary", "arbitrary"),
                                             vmem_limit_bytes=VMEM_LIMIT),
        name="mlstm",
    )(um, gc, gt, conv_w, ng)


def _hgrn_kernel(uh_ref, lb_ref, ng_ref, y_ref, bc_ref, k_ref, v_ref, w_ref, st_ref):
    R = H_BLOCK
    NS = R // H_SUB
    c = pl.program_id(1)

    @pl.when(c == 0)
    def _():
        st_ref[...] = jnp.zeros(st_ref.shape, F32)

    zpad = jnp.zeros((H_SUB, H_WIDTH), F32)
    bc_ref[0:H_SUB, :] = zpad
    k_ref[0:H_SUB, :] = zpad
    v_ref[0:H_SUB, :] = zpad

    lb = lb_ref[...]
    fp = uh_ref[0, :, H_WIDTH:2 * H_WIDTH]
    hq = uh_ref[0, :, 0:H_WIDTH]
    q = hq * jax.nn.sigmoid(hq)
    logf = jnp.log(lb + (1.0 - lb) * jax.nn.sigmoid(fp))
    k = (1.0 - lb) * jax.nn.sigmoid(-fp)
    v = uh_ref[0, :, 2 * H_WIDTH:3 * H_WIDTH]

    rowmod = lax.broadcasted_iota(jnp.int32, (R, H_WIDTH), 0) % H_SUB
    bc = logf
    shift = 1
    while shift < H_SUB:
        bc = bc + jnp.where(rowmod >= shift, pltpu.roll(bc, shift, 0), 0.0)
        shift *= 2
    bc_ref[H_SUB:, :] = bc
    k_ref[H_SUB:, :] = k
    v_ref[H_SUB:, :] = v

    ones_bd = _block_ones(LANES, HEAD_DIM)

    o = jnp.zeros((R, H_WIDTH), F32)
    for d in range(H_SUB):
        bs = bc_ref[H_SUB - d:H_SUB - d + R, :]
        ks = k_ref[H_SUB - d:H_SUB - d + R, :]
        vs = v_ref[H_SUB - d:H_SUB - d + R, :]
        pr = (q * ks * jnp.exp(jnp.where(rowmod >= d, bc - bs, NEG_BIG))).astype(BF16)
        red = jnp.concatenate([_dot(pr[:, 0:LANES], ones_bd), _dot(pr[:, LANES:], ones_bd)], axis=1)
        o = o + red * vs

    bl_rows = jnp.concatenate(
        [jnp.broadcast_to(bc[(j + 1) * H_SUB - 1:(j + 1) * H_SUB, :], (H_SUB, H_WIDTH)) for j in range(NS)], axis=0)
    qe = q * jnp.exp(bc)
    ke = k * jnp.exp(bl_rows - bc)
    tcol = lax.broadcasted_iota(jnp.int32, (LANES, R), 1) // H_SUB
    trow = lax.broadcasted_iota(jnp.int32, (R, LANES), 0) // H_SUB
    sr = lax.broadcasted_iota(jnp.int32, (LANES, LANES), 0) < HEAD_DIM
    s_block = sr == (lax.broadcasted_iota(jnp.int32, (LANES, LANES), 1) < HEAD_DIM)
    outs = []
    for p in range(2):
        sl = slice(p * LANES, (p + 1) * LANES)
        v_t = v[:, sl].T
        v_stack = jnp.concatenate([jnp.where(tcol == j, v_t, 0.0) for j in range(NS)], axis=0).astype(BF16)
        upd = _dot(v_stack, ke[:, sl].astype(BF16))
        st = st_ref[p]
        for j in range(NS):
            w_ref[:, j * LANES:(j + 1) * LANES] = st
            a_row = jnp.exp(bc[(j + 1) * H_SUB - 1:(j + 1) * H_SUB, sl])
            st = st * a_row + jnp.where(s_block, upd[j * LANES:(j + 1) * LANES, :], 0.0)
        st_ref[p] = st
        q_exp = jnp.concatenate([jnp.where(trow == j, qe[:, sl], 0.0) for j in range(NS)], axis=1).astype(BF16)
        outs.append(_nt_dot(q_exp, w_ref[...].astype(BF16)))
    o = o + jnp.concatenate(outs, axis=1)

    ms = jnp.concatenate([_group_mean(o[:, 0:LANES] * o[:, 0:LANES], ones_bd, HEAD_DIM),
                          _group_mean(o[:, LANES:] * o[:, LANES:], ones_bd, HEAD_DIM)], axis=1)
    hg = uh_ref[0, :, 3 * H_WIDTH:4 * H_WIDTH]
    y = (o * lax.rsqrt(ms + EPS) * ng_ref[...]) * (hg * jax.nn.sigmoid(hg))
    y_ref[0] = y.astype(y_ref.dtype)


def _hgrn(uh, lb, ng):
    b, t, _ = uh.shape
    ns = H_BLOCK // H_SUB
    pad = pltpu.VMEM((H_BLOCK + H_SUB, H_WIDTH), F32)
    return pl.pallas_call(
        _hgrn_kernel,
        grid=(b, t // H_BLOCK),
        in_specs=[pl.BlockSpec((1, H_BLOCK, 4 * H_WIDTH), lambda i, c: (i, c, 0)),
                  pl.BlockSpec((1, H_WIDTH), lambda i, c: (0, 0)),
                  pl.BlockSpec((1, H_WIDTH), lambda i, c: (0, 0))],
        out_specs=pl.BlockSpec((1, H_BLOCK, H_WIDTH), lambda i, c: (i, c, 0)),
        out_shape=jax.ShapeDtypeStruct((b, t, H_WIDTH), BF16),
        scratch_shapes=[pad, pad, pad,
                        pltpu.VMEM((LANES, ns * LANES), F32),
                        pltpu.VMEM((2, LANES, LANES), F32)],
        compiler_params=pltpu.CompilerParams(dimension_semantics=("arbitrary", "arbitrary"),
                                             vmem_limit_bytes=VMEM_LIMIT),
        name="hgrn2",
    )(uh, lb, ng)


def _attn_prep_kernel(ua_ref, qg_ref, kg_ref, cos_ref, s1_ref, s2_ref, q_ref, k_ref, v_ref):
    ones_bd = _block_ones(LANES, HEAD_DIM)
    reps = A_WIDTH // LANES
    cos = jnp.concatenate([cos_ref[...]] * reps, axis=1)
    s1 = jnp.concatenate([s1_ref[...]] * reps, axis=1)
    s2 = jnp.concatenate([s2_ref[...]] * reps, axis=1)

    def norm_rope(x, g):
        ms = jnp.concatenate(
            [_group_mean(x[:, i * LANES:(i + 1) * LANES] * x[:, i * LANES:(i + 1) * LANES], ones_bd, HEAD_DIM)
             for i in range(reps)], axis=1)
        xn = x * lax.rsqrt(ms + EPS) * g
        return xn * cos + pltpu.roll(xn, A_WIDTH - ROT_DIM // 2, 1) * s1 + pltpu.roll(xn, ROT_DIM // 2, 1) * s2

    q = norm_rope(ua_ref[0, :, 0:A_WIDTH], qg_ref[...]) * (HEAD_DIM ** -0.5)
    q_ref[0] = q.astype(BF16)
    k_ref[0] = norm_rope(ua_ref[0, :, A_WIDTH:2 * A_WIDTH], kg_ref[...]).astype(BF16)
    v_ref[0] = ua_ref[0, :, 2 * A_WIDTH:].astype(BF16)


def _attn_prep(ua, qg, kg, cos, s1, s2):
    b, t, _ = ua.shape
    tab = pl.BlockSpec((PREP_TILE, LANES), lambda i, c: (c, 0))
    vec = pl.BlockSpec((1, A_WIDTH), lambda i, c: (0, 0))
    out = pl.BlockSpec((1, PREP_TILE, A_WIDTH), lambda i, c: (i, c, 0))
    shp = jax.ShapeDtypeStruct((b, t, A_WIDTH), BF16)
    return pl.pallas_call(
        _attn_prep_kernel,
        grid=(b, t // PREP_TILE),
        in_specs=[pl.BlockSpec((1, PREP_TILE, 3 * A_WIDTH), lambda i, c: (i, c, 0)), vec, vec, tab, tab, tab],
        out_specs=[out, out, out],
        out_shape=[shp, shp, shp],
        compiler_params=pltpu.CompilerParams(dimension_semantics=("arbitrary", "arbitrary"),
                                             vmem_limit_bytes=VMEM_LIMIT),
        name="attn_prep",
    )(ua, qg, kg, cos, s1, s2)


def _attn_kernel(lam_ref, q_ref, k_ref, v_ref, sg_ref, y_ref, *, out_scale):
    qi = pl.program_id(2)
    q = q_ref[0]
    lane = lax.broadcasted_iota(jnp.int32, (Q_TILE, LANES), 1)
    zero = jnp.zeros_like(q)
    qc = (jnp.where(lane < HEAD_DIM, q, zero), jnp.where(lane >= HEAD_DIM, q, zero))
    row = lax.broadcasted_iota(jnp.int32, (Q_TILE, K_TILE), 0) + qi * Q_TILE
    col = lax.broadcasted_iota(jnp.int32, (Q_TILE, K_TILE), 1)

    def body(j, carry):
        kb = k_ref[0, pl.ds(j * K_TILE, K_TILE), :]
        vb = v_ref[0, pl.ds(j * K_TILE, K_TILE), :]
        visible = row >= col + j * K_TILE
        new = []
        for comp in range(2):
            m, l, acc = carry[3 * comp:3 * comp + 3]
            s = jnp.where(visible, _nt_dot(qc[comp], kb), NEG_BIG)
            m_new = jnp.maximum(m, jnp.max(s, axis=1, keepdims=True))
            alpha = jnp.exp(m - m_new)
            pexp = jnp.exp(s - m_new)
            l = alpha * l + jnp.sum(pexp, axis=1, keepdims=True)
            acc = alpha * acc + _dot(pexp.astype(BF16), vb)
            new += [m_new, l, acc]
        return tuple(new)

    init = []
    for _ in range(2):
        init += [jnp.full((Q_TILE, 1), NEG_BIG, F32), jnp.zeros((Q_TILE, 1), F32), jnp.zeros((Q_TILE, LANES), F32)]
    n_kv = (qi * Q_TILE) // K_TILE + Q_TILE // K_TILE
    m0, l0, a0, m1, l1, a1 = lax.fori_loop(0, n_kv, body, tuple(init))
    o = a0 / l0 - lam_ref[0] * (a1 / l1)
    ms = jnp.mean(o * o, axis=-1, keepdims=True)
    y_ref[0] = ((o * lax.rsqrt(ms + EPS) * sg_ref[...]) * out_scale).astype(y_ref.dtype)


def _attn(lam, qh, kh, vh, sg, out_scale):
    b, t, _ = qh.shape
    qspec = pl.BlockSpec((1, Q_TILE, LANES), lambda i, h, j: (i, j, h))
    kvspec = pl.BlockSpec((1, t, LANES), lambda i, h, j: (i, 0, h))
    return pl.pallas_call(
        functools.partial(_attn_kernel, out_scale=out_scale),
        grid=(b, A_HEADS, t // Q_TILE),
        in_specs=[pl.BlockSpec(memory_space=pltpu.SMEM), qspec, kvspec, kvspec,
                  pl.BlockSpec((1, LANES), lambda i, h, j: (0, 0))],
        out_specs=qspec,
        out_shape=jax.ShapeDtypeStruct((b, t, A_WIDTH), BF16),
        compiler_params=pltpu.CompilerParams(dimension_semantics=("arbitrary", "arbitrary", "arbitrary"),
                                             vmem_limit_bytes=VMEM_LIMIT),
        name="diff_attn",
    )(lam, qh, kh, vh, sg)


def _rope_tables(t):
    half = ROT_DIM // 2
    inv = ROPE_THETA ** (-jnp.arange(half, dtype=F32) / half)
    ang = jnp.arange(t, dtype=F32)[:, None] * inv[None, :]
    cos, sin = jnp.cos(ang), jnp.sin(ang)
    pad = jnp.zeros((t, HEAD_DIM - ROT_DIM), F32)
    zero = jnp.zeros((t, half), F32)
    c = jnp.concatenate([cos, cos, pad + 1.0], axis=1)
    s1 = jnp.concatenate([-sin, zero, pad], axis=1)
    s2 = jnp.concatenate([zero, sin, pad], axis=1)
    tile = lambda a: jnp.concatenate([a, a], axis=1)
    return tile(c), tile(s1), tile(s2)


def kernel(x, norm1_g, w_in, mlstm_conv_w, mlstm_gate_b, mlstm_norm_g, diff_q_norm_g, diff_k_norm_g,
           diff_lambda, diff_subln_g, hgrn_lb_param, hgrn_norm_g, w_out, norm2_g, w_up, w_down):
    b, t, d = x.shape
    n = b * t
    assert d == D_MODEL and n % ROW_TILE == 0 and t % max(M_CHUNK, H_BLOCK, Q_TILE, K_TILE, PREP_TILE) == 0

    sm = jax.nn.softmax(hgrn_lb_param.astype(F32), axis=0)
    lower_bounds = jnp.cumsum(sm, axis=0) - sm[0]
    cos, s1, s2 = _rope_tables(t)
    c_m, c_g, c_a = 4 * M_WIDTH, 4 * M_WIDTH + 8, 4 * M_WIDTH + 8 + 3 * A_WIDTH

    x2 = x.reshape(n, d)
    for l in range(DEPTH):
        lambda_init = 0.8 - 0.6 * math.exp(-0.3 * l)
        w = w_in[l]
        wg = jnp.pad(w[:, c_m:c_g], ((0, 0), (0, LANES - 8))).astype(BF16)
        wgt = jnp.pad(w[:, c_m:c_g].T, ((0, 8), (0, 0))).astype(BF16)
        gate_b = mlstm_gate_b[l].astype(F32).reshape(1, 8)
        bg = jnp.pad(gate_b, ((0, 0), (0, LANES - 8)))
        bgt = jnp.broadcast_to(jnp.pad(gate_b.reshape(8, 1), ((0, 8), (0, 0))), (16, LANES))
        um, ua, uh, gc, gt = _inproj(
            x2, norm1_g[l].reshape(1, d), w[:, :c_m].astype(BF16), w[:, c_g:c_a].astype(BF16),
            w[:, c_a:].astype(BF16), wg, wgt, bg, bgt)

        ym = _mlstm(um.reshape(b, t, -1), gc.reshape(b, t, LANES), gt, mlstm_conv_w[l].astype(F32),
                    jnp.tile(mlstm_norm_g[l].astype(F32), 2).reshape(1, LANES))

        qh, kh, vh = _attn_prep(ua.reshape(b, t, -1),
                                jnp.tile(diff_q_norm_g[l].astype(F32), A_WIDTH // HEAD_DIM).reshape(1, A_WIDTH),
                                jnp.tile(diff_k_norm_g[l].astype(F32), A_WIDTH // HEAD_DIM).reshape(1, A_WIDTH),
                                cos, s1, s2)
        lv = diff_lambda[l].astype(F32)
        lam = jnp.exp(jnp.sum(lv[0] * lv[1])) - jnp.exp(jnp.sum(lv[2] * lv[3])) + lambda_init
        ya = _attn(lam.reshape(1), qh, kh, vh, diff_subln_g[l].astype(F32).reshape(1, LANES), 1.0 - lambda_init)

        yh = _hgrn(uh.reshape(b, t, -1), lower_bounds[l].reshape(1, H_WIDTH),
                   jnp.tile(hgrn_norm_g[l].astype(F32), H_WIDTH // HEAD_DIM).reshape(1, H_WIDTH))

        x2 = _post(x2, ym.reshape(n, -1), ya.reshape(n, -1), yh.reshape(n, -1), w_out[l].astype(BF16),
                   norm2_g[l].reshape(1, d), w_up[l].astype(BF16), w_down[l].astype(BF16))
    return x2.reshape(b, t, d)
```

```python
import functools
import math

import jax
import jax.numpy as jnp
from jax import lax
from jax.experimental import pallas as pl
from jax.experimental.pallas import tpu as pltpu

F32 = jnp.float32
BF16 = jnp.bfloat16

D_MODEL = 1024
DEPTH = 2
D_FF = 4 * D_MODEL
EPS = 1e-6
NEG_BIG = -1e30
HEAD_DIM = 64
M_WIDTH = 256
M_CONV = 4
A_WIDTH = 512
A_HEADS = 4
ROT_DIM = 16
ROPE_THETA = 500000.0
H_WIDTH = 256
LANES = 128

VMEM_LIMIT = 56 * 1024 * 1024

ROW_TILE = 512
COL_CHUNK = 512
M_CHUNK = 128
H_BLOCK = 256
H_SUB = 16
Q_TILE = 256
K_TILE = 512
PREP_TILE = 512


def _nt_dot(a, b):
    return lax.dot_general(a, b, (((1,), (1,)), ((), ())), preferred_element_type=F32)


def _dot(a, b):
    return jnp.dot(a, b, preferred_element_type=F32)


def _block_ones(n, group):
    r = lax.broadcasted_iota(jnp.int32, (n, n), 0) // group
    c = lax.broadcasted_iota(jnp.int32, (n, n), 1) // group
    return jnp.where(r == c, 1.0, 0.0).astype(BF16)


def _group_mean(xsq, ones_bd, group):
    hi = xsq.astype(BF16)
    lo = (xsq - hi.astype(F32)).astype(BF16)
    return (_dot(hi, ones_bd) + _dot(lo, ones_bd)) * (1.0 / group)


def _log_sigmoid(x):
    return jnp.minimum(x, 0.0) - jnp.log1p(jnp.exp(-jnp.abs(x)))


def _inproj_kernel(x_ref, g_ref, wm_ref, wa_ref, wh_ref, wg_ref, wgt_ref, bg_ref, bgt_ref,
                   um_ref, ua_ref, uh_ref, gc_ref, gt_ref):
    x = x_ref[...]
    ms = jnp.mean(x * x, axis=-1, keepdims=True)
    h = (x * lax.rsqrt(ms + EPS) * g_ref[...]).astype(BF16)
    for w_ref, o_ref in ((wm_ref, um_ref), (wa_ref, ua_ref), (wh_ref, uh_ref)):
        for c in range(w_ref.shape[1] // COL_CHUNK):
            sl = slice(c * COL_CHUNK, (c + 1) * COL_CHUNK)
            o_ref[:, sl] = _dot(h, w_ref[:, sl])
    gc_ref[...] = _dot(h, wg_ref[...]) + bg_ref[...]
    gt_ref[...] = _nt_dot(wgt_ref[...], h) + bgt_ref[:, 0:1]


def _inproj(x2, g, wm, wa, wh, wg, wgt, bg, bgt):
    n = x2.shape[0]
    const = lambda shape: pl.BlockSpec(shape, lambda i: (0, 0))
    rows = lambda width: pl.BlockSpec((ROW_TILE, width), lambda i: (i, 0))
    return pl.pallas_call(
        _inproj_kernel,
        grid=(n // ROW_TILE,),
        in_specs=[rows(D_MODEL), const((1, D_MODEL)), const(wm.shape), const(wa.shape), const(wh.shape),
                  const(wg.shape), const(wgt.shape), const(bg.shape), const(bgt.shape)],
        out_specs=[rows(wm.shape[1]), rows(wa.shape[1]), rows(wh.shape[1]), rows(LANES),
                   pl.BlockSpec((16, ROW_TILE), lambda i: (0, i))],
        out_shape=[jax.ShapeDtypeStruct((n, wm.shape[1]), F32),
                   jax.ShapeDtypeStruct((n, wa.shape[1]), F32),
                   jax.ShapeDtypeStruct((n, wh.shape[1]), F32),
                   jax.ShapeDtypeStruct((n, LANES), F32),
                   jax.ShapeDtypeStruct((16, n), F32)],
        compiler_params=pltpu.CompilerParams(dimension_semantics=("arbitrary",), vmem_limit_bytes=VMEM_LIMIT),
        name="inproj",
    )(x2, g, wm, wa, wh, wg, wgt, bg, bgt)


def _post_kernel(x_ref, ym_ref, ya_ref, yh_ref, wo_ref, g2_ref, wu_ref, wd_ref, o_ref):
    x1 = x_ref[...]
    x1 = x1 + _dot(ym_ref[...], wo_ref[0:M_WIDTH, :])
    x1 = x1 + _dot(ya_ref[...], wo_ref[M_WIDTH:M_WIDTH + A_WIDTH, :])
    x1 = x1 + _dot(yh_ref[...], wo_ref[M_WIDTH + A_WIDTH:, :])
    ms = jnp.mean(x1 * x1, axis=-1, keepdims=True)
    h2 = (x1 * lax.rsqrt(ms + EPS) * g2_ref[...]).astype(BF16)
    o_ref[...] = x1
    for c in range(D_FF // COL_CHUNK):
        sl = slice(c * COL_CHUNK, (c + 1) * COL_CHUNK)
        a = jnp.maximum(_dot(h2, wu_ref[:, sl]), 0.0)
        o_ref[...] += _dot((a * a).astype(BF16), wd_ref[sl, :])


def _post(x2, ym, ya, yh, wo, g2, wu, wd):
    n = x2.shape[0]
    const = lambda shape: pl.BlockSpec(shape, lambda i: (0, 0), pipeline_mode=pl.Buffered(1))
    rows = lambda width: pl.BlockSpec((ROW_TILE, width), lambda i: (i, 0))
    return pl.pallas_call(
        _post_kernel,
        grid=(n // ROW_TILE,),
        in_specs=[rows(D_MODEL), rows(M_WIDTH), rows(A_WIDTH), rows(H_WIDTH),
                  const(wo.shape), const((1, D_MODEL)), const(wu.shape), const(wd.shape)],
        out_specs=rows(D_MODEL),
        out_shape=jax.ShapeDtypeStruct((n, D_MODEL), F32),
        compiler_params=pltpu.CompilerParams(dimension_semantics=("arbitrary",), vmem_limit_bytes=VMEM_LIMIT),
        name="post",
    )(x2, ym, ya, yh, wo, g2, wu, wd)


def _mlstm_kernel(um_ref, gc_ref, gt_ref, cw_ref, ng_ref, y_ref, xpad_ref, gn_ref, m_ref):
    L = M_CHUNK
    c = pl.program_id(1)

    @pl.when(c == 0)
    def _():
        xpad_ref[0:8, :] = jnp.zeros((8, 2 * M_WIDTH), F32)
        gn_ref[...] = jnp.zeros(gn_ref.shape, F32)
        m_ref[...] = jnp.zeros(m_ref.shape, F32)

    xin = um_ref[0, :, 0:2 * M_WIDTH]
    xpad_ref[8:8 + L, :] = xin
    conv = xpad_ref[5:5 + L, :] * cw_ref[0:1, :]
    conv = conv + xpad_ref[6:6 + L, :] * cw_ref[1:2, :]
    conv = conv + xpad_ref[7:7 + L, :] * cw_ref[2:3, :]
    conv = conv + xin * cw_ref[3:4, :]
    xpad_ref[0:8, :] = xin[L - 8:L, :]
    qk = conv * jax.nn.sigmoid(conv)
    q_all = qk[:, 0:M_WIDTH]
    k_all = qk[:, M_WIDTH:] * (HEAD_DIM ** -0.5)
    v_all = um_ref[0, :, 2 * M_WIDTH:3 * M_WIDTH]
    o_all = um_ref[0, :, 3 * M_WIDTH:4 * M_WIDTH]

    gcol = gc_ref[0]
    grow = gt_ref[...]
    logf_col = _log_sigmoid(gcol)
    logf_row = _log_sigmoid(grow)

    ti = lax.broadcasted_iota(jnp.int32, (L, L), 0)
    si = lax.broadcasted_iota(jnp.int32, (L, L), 1)
    causal = si <= ti
    lane = lax.broadcasted_iota(jnp.int32, (L, LANES), 1)
    left = lane < HEAD_DIM
    ones_bd = _block_ones(LANES, HEAD_DIM)
    gr = lax.broadcasted_iota(jnp.int32, (LANES, 2 * LANES), 0)
    gc_i = lax.broadcasted_iota(jnp.int32, (LANES, 2 * LANES), 1)
    g_rowleft = gr < HEAD_DIM
    g_block = g_rowleft == ((gc_i % LANES) < HEAD_DIM)

    for p in range(2):
        sl = slice(p * LANES, (p + 1) * LANES)
        q_pair = q_all[:, sl]
        k_pair = k_all[:, sl]
        v1 = jnp.concatenate([v_all[:, sl], jnp.ones((L, LANES), F32)], axis=1).astype(BF16)
        k_bf = k_pair.astype(BF16)
        intra, scale, einv, wcol, decay = [], [], [], [], []
        for hh in range(2):
            h = 2 * p + hh
            m_prev = m_ref[h:h + 1, 0:1]
            i_row = grow[h:h + 1, :]
            i_col = gcol[:, h:h + 1]
            lf_row = logf_row[4 + h:5 + h, :]
            lf_col = logf_col[:, 4 + h:5 + h]
            b_col = jnp.sum(jnp.where(causal, lf_row, 0.0), axis=1, keepdims=True)
            b_row = jnp.sum(jnp.where(ti <= si, lf_col, 0.0), axis=0, keepdims=True)
            a_row = i_row - b_row
            a_col = i_col - b_col
            dm = jnp.where(causal, a_row, NEG_BIG)
            m_col = jnp.maximum(jnp.max(dm, axis=1, keepdims=True), m_prev)
            e = jnp.exp(dm - m_col)
            qm = jnp.where(left if hh == 0 else jnp.logical_not(left), q_pair, 0.0).astype(BF16)
            s = _nt_dot(qm, k_bf) * e
            intra.append(_dot(s.astype(BF16), v1))
            m_last = m_col[L - 1:L, :]
            scale.append(jnp.exp(m_prev - m_col))
            einv.append(jnp.exp(-(b_col + m_col)))
            wcol.append(jnp.exp(a_col - m_last))
            decay.append(jnp.exp(m_prev - m_last))
            m_ref[h:h + 1, :] = jnp.broadcast_to(b_col[L - 1:L, :] + m_last, (1, LANES))

        gn = gn_ref[p]
        inter = _dot(q_pair.astype(BF16), gn.astype(BF16))
        sc = jnp.where(left, scale[0], scale[1])
        num = jnp.where(left, intra[0][:, :LANES], intra[1][:, :LANES]) + sc * inter[:, :LANES]
        den = jnp.where(left, intra[0][:, LANES:], intra[1][:, LANES:]) + sc * inter[:, LANES:]
        hv = num / jnp.maximum(jnp.abs(den), jnp.where(left, einv[0], einv[1]))

        kw = k_pair * jnp.where(left, wcol[0], wcol[1])
        upd = _dot(kw.T.astype(BF16), v1)
        gn_ref[p] = jnp.where(g_rowleft, decay[0], decay[1]) * gn + jnp.where(g_block, upd, 0.0)

        ms = _group_mean(hv * hv, ones_bd, HEAD_DIM)
        y = jax.nn.sigmoid(o_all[:, sl]) * (hv * lax.rsqrt(ms + EPS) * ng_ref[...])
        y_ref[0, :, sl] = y.astype(y_ref.dtype)


def _mlstm(um, gc, gt, conv_w, ng):
    b, t, _ = um.shape
    nc = t // M_CHUNK
    return pl.pallas_call(
        _mlstm_kernel,
        grid=(b, nc),
        in_specs=[pl.BlockSpec((1, M_CHUNK, 4 * M_WIDTH), lambda i, c: (i, c, 0)),
                  pl.BlockSpec((1, M_CHUNK, LANES), lambda i, c: (i, c, 0)),
                  pl.BlockSpec((16, M_CHUNK), lambda i, c: (0, i * nc + c)),
                  pl.BlockSpec((M_CONV, 2 * M_WIDTH), lambda i, c: (0, 0)),
                  pl.BlockSpec((1, LANES), lambda i, c: (0, 0))],
        out_specs=pl.BlockSpec((1, M_CHUNK, M_WIDTH), lambda i, c: (i, c, 0)),
        out_shape=jax.ShapeDtypeStruct((b, t, M_WIDTH), BF16),
        scratch_shapes=[pltpu.VMEM((M_CHUNK + 8, 2 * M_WIDTH), F32),
                        pltpu.VMEM((2, LANES, 2 * LANES), F32),
                        pltpu.VMEM((8, LANES), F32)],
        compiler_params=pltpu.CompilerParams(dimension_semantics=("arbitrary", "arbitrary"),
                                             vmem_limit_bytes=VMEM_LIMIT),
        name="mlstm",
    )(um, gc, gt, conv_w, ng)


def _hgrn_kernel(uh_ref, lb_ref, ng_ref, y_ref, bc_ref, k_ref, v_ref, w_ref, st_ref):
    R = H_BLOCK
    NS = R // H_SUB
    c = pl.program_id(1)

    @pl.when(c == 0)
    def _():
        st_ref[...] = jnp.zeros(st_ref.shape, F32)

    zpad = jnp.zeros((H_SUB, H_WIDTH), F32)
    bc_ref[0:H_SUB, :] = zpad
    k_ref[0:H_SUB, :] = zpad
    v_ref[0:H_SUB, :] = zpad

    lb = lb_ref[...]
    fp = uh_ref[0, :, H_WIDTH:2 * H_WIDTH]
    hq = uh_ref[0, :, 0:H_WIDTH]
    q = hq * jax.nn.sigmoid(hq)
    logf = jnp.log(lb + (1.0 - lb) * jax.nn.sigmoid(fp))
    k = (1.0 - lb) * jax.nn.sigmoid(-fp)
    v = uh_ref[0, :, 2 * H_WIDTH:3 * H_WIDTH]

    rowmod = lax.broadcasted_iota(jnp.int32, (R, H_WIDTH), 0) % H_SUB
    bc = logf
    shift = 1
    while shift < H_SUB:
        bc = bc + jnp.where(rowmod >= shift, pltpu.roll(bc, shift, 0), 0.0)
        shift *= 2
    bc_ref[H_SUB:, :] = bc
    k_ref[H_SUB:, :] = k
    v_ref[H_SUB:, :] = v

    ones_bd = _block_ones(LANES, HEAD_DIM)

    o = jnp.zeros((R, H_WIDTH), F32)
    for d in range(H_SUB):
        bs = bc_ref[H_SUB - d:H_SUB - d + R, :]
        ks = k_ref[H_SUB - d:H_SUB - d + R, :]
        vs = v_ref[H_SUB - d:H_SUB - d + R, :]
        pr = (q * ks * jnp.exp(jnp.where(rowmod >= d, bc - bs, NEG_BIG))).astype(BF16)
        red = jnp.concatenate([_dot(pr[:, 0:LANES], ones_bd), _dot(pr[:, LANES:], ones_bd)], axis=1)
        o = o + red * vs

    bl_rows = jnp.concatenate(
        [jnp.broadcast_to(bc[(j + 1) * H_SUB - 1:(j + 1) * H_SUB, :], (H_SUB, H_WIDTH)) for j in range(NS)], axis=0)
    qe = q * jnp.exp(bc)
    ke = k * jnp.exp(bl_rows - bc)
    tcol = lax.broadcasted_iota(jnp.int32, (LANES, R), 1) // H_SUB
    trow = lax.broadcasted_iota(jnp.int32, (R, LANES), 0) // H_SUB
    sr = lax.broadcasted_iota(jnp.int32, (LANES, LANES), 0) < HEAD_DIM
    s_block = sr == (lax.broadcasted_iota(jnp.int32, (LANES, LANES), 1) < HEAD_DIM)
    outs = []
    for p in range(2):
        sl = slice(p * LANES, (p + 1) * LANES)
        v_t = v[:, sl].T
        v_stack = jnp.concatenate([jnp.where(tcol == j, v_t, 0.0) for j in range(NS)], axis=0).astype(BF16)
        upd = _dot(v_stack, ke[:, sl].astype(BF16))
        st = st_ref[p]
        for j in range(NS):
            w_ref[:, j * LANES:(j + 1) * LANES] = st
            a_row = jnp.exp(bc[(j + 1) * H_SUB - 1:(j + 1) * H_SUB, sl])
            st = st * a_row + jnp.where(s_block, upd[j * LANES:(j + 1) * LANES, :], 0.0)
        st_ref[p] = st
        q_exp = jnp.concatenate([jnp.where(trow == j, qe[:, sl], 0.0) for j in range(NS)], axis=1).astype(BF16)
        outs.append(_nt_dot(q_exp, w_ref[...].astype(BF16)))
    o = o + jnp.concatenate(outs, axis=1)

    ms = jnp.concatenate([_group_mean(o[:, 0:LANES] * o[:, 0:LANES], ones_bd, HEAD_DIM),
                          _group_mean(o[:, LANES:] * o[:, LANES:], ones_bd, HEAD_DIM)], axis=1)
    hg = uh_ref[0, :, 3 * H_WIDTH:4 * H_WIDTH]
    y = (o * lax.rsqrt(ms + EPS) * ng_ref[...]) * (hg * jax.nn.sigmoid(hg))
    y_ref[0] = y.astype(y_ref.dtype)


def _hgrn(uh, lb, ng):
    b, t, _ = uh.shape
    ns = H_BLOCK // H_SUB
    pad = pltpu.VMEM((H_BLOCK + H_SUB, H_WIDTH), F32)
    return pl.pallas_call(
        _hgrn_kernel,
        grid=(b, t // H_BLOCK),
        in_specs=[pl.BlockSpec((1, H_BLOCK, 4 * H_WIDTH), lambda i, c: (i, c, 0)),
                  pl.BlockSpec((1, H_WIDTH), lambda i, c: (0, 0)),
                  pl.BlockSpec((1, H_WIDTH), lambda i, c: (0, 0))],
        out_specs=pl.BlockSpec((1, H_BLOCK, H_WIDTH), lambda i, c: (i, c, 0)),
        out_shape=jax.ShapeDtypeStruct((b, t, H_WIDTH), BF16),
        scratch_shapes=[pad, pad, pad,
                        pltpu.VMEM((LANES, ns * LANES), F32),
                        pltpu.VMEM((2, LANES, LANES), F32)],
        compiler_params=pltpu.CompilerParams(dimension_semantics=("arbitrary", "arbitrary"),
                                             vmem_limit_bytes=VMEM_LIMIT),
        name="hgrn2",
    )(uh, lb, ng)


def _attn_prep_kernel(ua_ref, qg_ref, kg_ref, cos_ref, s1_ref, s2_ref, q_ref, k_ref, v_ref):
    ones_bd = _block_ones(LANES, HEAD_DIM)
    reps = A_WIDTH // LANES
    cos = jnp.concatenate([cos_ref[...]] * reps, axis=1)
    s1 = jnp.concatenate([s1_ref[...]] * reps, axis=1)
    s2 = jnp.concatenate([s2_ref[...]] * reps, axis=1)

    def norm_rope(x, g):
        ms = jnp.concatenate(
            [_group_mean(x[:, i * LANES:(i + 1) * LANES] * x[:, i * LANES:(i + 1) * LANES], ones_bd, HEAD_DIM)
             for i in range(reps)], axis=1)
        xn = x * lax.rsqrt(ms + EPS) * g
        return xn * cos + pltpu.roll(xn, A_WIDTH - ROT_DIM // 2, 1) * s1 + pltpu.roll(xn, ROT_DIM // 2, 1) * s2

    q = norm_rope(ua_ref[0, :, 0:A_WIDTH], qg_ref[...]) * (HEAD_DIM ** -0.5 * math.log2(math.e))
    q_ref[0] = q.astype(BF16)
    k_ref[0] = norm_rope(ua_ref[0, :, A_WIDTH:2 * A_WIDTH], kg_ref[...]).astype(BF16)
    v_ref[0] = ua_ref[0, :, 2 * A_WIDTH:].astype(BF16)


def _attn_prep(ua, qg, kg, cos, s1, s2):
    b, t, _ = ua.shape
    tab = pl.BlockSpec((PREP_TILE, LANES), lambda i, c: (c, 0))
    vec = pl.BlockSpec((1, A_WIDTH), lambda i, c: (0, 0))
    out = pl.BlockSpec((1, PREP_TILE, A_WIDTH), lambda i, c: (i, c, 0))
    shp = jax.ShapeDtypeStruct((b, t, A_WIDTH), BF16)
    return pl.pallas_call(
        _attn_prep_kernel,
        grid=(b, t // PREP_TILE),
        in_specs=[pl.BlockSpec((1, PREP_TILE, 3 * A_WIDTH), lambda i, c: (i, c, 0)), vec, vec, tab, tab, tab],
        out_specs=[out, out, out],
        out_shape=[shp, shp, shp],
        compiler_params=pltpu.CompilerParams(dimension_semantics=("arbitrary", "arbitrary"),
                                             vmem_limit_bytes=VMEM_LIMIT),
        name="attn_prep",
    )(ua, qg, kg, cos, s1, s2)


def _attn_kernel(tab_ref, lam_ref, q_ref, k_ref, v_ref, sg_ref, y_ref,
                 qs_ref, d_ref, s0_ref, s1_ref, p0_ref, p1_ref, a0_ref, a1_ref, m_ref, acc_ref, *,
                 out_scale, n_steps):
    n_q = q_ref.shape[1] // Q_TILE
    lane = lax.broadcasted_iota(jnp.int32, (Q_TILE, LANES), 1)
    for i in range(n_q):
        q = q_ref[0, i * Q_TILE:(i + 1) * Q_TILE, :]
        zero = jnp.zeros_like(q)
        qs_ref[i, 0:Q_TILE, :] = jnp.where(lane < HEAD_DIM, q, zero)
        qs_ref[i, Q_TILE:, :] = jnp.where(lane >= HEAD_DIM, q, zero)
    r = lax.broadcasted_iota(jnp.int32, (2 * Q_TILE, K_TILE), 0)
    d_ref[...] = jnp.where(r >= Q_TILE, r - Q_TILE, r) - lax.broadcasted_iota(jnp.int32, (2 * Q_TILE, K_TILE), 1)
    ones = jnp.ones((K_TILE, LANES), BF16)

    def kstart(t):
        return pl.multiple_of(tab_ref[1, t] * K_TILE, K_TILE)

    def scores(t, s_ref):
        s_ref[...] = _nt_dot(qs_ref[tab_ref[0, t]], k_ref[0, pl.ds(kstart(t), K_TILE), :])

    def numerators(t, s_ref, p_ref, a_ref):
        visible = d_ref[...] >= kstart(t) - tab_ref[0, t] * Q_TILE
        s = jnp.where(visible, s_ref[...], NEG_BIG)
        m_old = m_ref[...]
        m_new = jnp.maximum(m_old, jnp.max(s, axis=1, keepdims=True))
        p_ref[...] = jnp.exp2(s - m_new).astype(BF16)
        a_ref[...] = jnp.exp2(m_old - m_new)
        m_ref[...] = jnp.where(tab_ref[2, t] == 1, NEG_BIG, m_new)

    def values(t, p_ref, a_ref):
        v1 = jnp.concatenate([v_ref[0, pl.ds(kstart(t), K_TILE), :], ones], axis=1)
        acc_ref[...] = a_ref[...] * acc_ref[...] + _dot(p_ref[...], v1)

        @pl.when(tab_ref[2, t] == 1)
        def _():
            acc = acc_ref[...]
            o = (acc[:Q_TILE, :LANES] / acc[:Q_TILE, LANES:]
                 - lam_ref[0] * (acc[Q_TILE:, :LANES] / acc[Q_TILE:, LANES:]))
            ms = jnp.mean(o * o, axis=-1, keepdims=True)
            y = (o * lax.rsqrt(ms + EPS) * sg_ref[...]) * out_scale
            y_ref[0, pl.ds(pl.multiple_of(tab_ref[0, t] * Q_TILE, Q_TILE), Q_TILE), :] = y.astype(y_ref.dtype)

    m_ref[...] = jnp.full(m_ref.shape, NEG_BIG, F32)
    acc_ref[...] = jnp.zeros(acc_ref.shape, F32)
    scores(0, s0_ref)
    scores(1, s1_ref)
    numerators(0, s0_ref, p0_ref, a0_ref)

    def two_steps(i, carry):
        t = 2 * i
        scores(t + 2, s0_ref)
        numerators(t + 1, s1_ref, p1_ref, a1_ref)
        values(t, p0_ref, a0_ref)
        scores(t + 3, s1_ref)
        numerators(t + 2, s0_ref, p0_ref, a0_ref)
        values(t + 1, p1_ref, a1_ref)
        return carry

    lax.fori_loop(0, n_steps // 2, two_steps, 0)


def _attn_schedule(t):
    cols = []
    for qi in range(t // Q_TILE):
        n_kv = (qi * Q_TILE) // K_TILE + 1
        cols += [(qi, kj, int(kj == n_kv - 1)) for kj in range(n_kv)]
    assert len(cols) % 2 == 0
    n_steps = len(cols)
    cols += [(0, 0, 0)] * 2
    return n_steps, jnp.asarray(cols, jnp.int32).T


def _attn(lam, qh, kh, vh, sg, out_scale):
    b, t, _ = qh.shape
    n_steps, tab = _attn_schedule(t)
    seq = pl.BlockSpec((1, t, LANES), lambda i, h: (i, 0, h))
    smem = pl.BlockSpec(memory_space=pltpu.SMEM)
    rows = 2 * Q_TILE
    return pl.pallas_call(
        functools.partial(_attn_kernel, out_scale=out_scale, n_steps=n_steps),
        grid=(b, A_HEADS),
        in_specs=[smem, smem, seq, seq, seq, pl.BlockSpec((1, LANES), lambda i, h: (0, 0))],
        out_specs=seq,
        out_shape=jax.ShapeDtypeStruct((b, t, A_WIDTH), BF16),
        scratch_shapes=[pltpu.VMEM((t // Q_TILE, rows, LANES), BF16),
                        pltpu.VMEM((rows, K_TILE), jnp.int32),
                        pltpu.VMEM((rows, K_TILE), F32), pltpu.VMEM((rows, K_TILE), F32),
                        pltpu.VMEM((rows, K_TILE), BF16), pltpu.VMEM((rows, K_TILE), BF16),
                        pltpu.VMEM((rows, 1), F32), pltpu.VMEM((rows, 1), F32),
                        pltpu.VMEM((rows, 1), F32),
                        pltpu.VMEM((rows, 2 * LANES), F32)],
        compiler_params=pltpu.CompilerParams(dimension_semantics=("arbitrary", "arbitrary"),
                                             vmem_limit_bytes=VMEM_LIMIT),
        name="diff_attn",
    )(tab, lam, qh, kh, vh, sg)


def _rope_tables(t):
    half = ROT_DIM // 2
    inv = ROPE_THETA ** (-jnp.arange(half, dtype=F32) / half)
    ang = jnp.arange(t, dtype=F32)[:, None] * inv[None, :]
    cos, sin = jnp.cos(ang), jnp.sin(ang)
    pad = jnp.zeros((t, HEAD_DIM - ROT_DIM), F32)
    zero = jnp.zeros((t, half), F32)
    c = jnp.concatenate([cos, cos, pad + 1.0], axis=1)
    s1 = jnp.concatenate([-sin, zero, pad], axis=1)
    s2 = jnp.concatenate([zero, sin, pad], axis=1)
    tile = lambda a: jnp.concatenate([a, a], axis=1)
    return tile(c), tile(s1), tile(s2)


def kernel(x, norm1_g, w_in, mlstm_conv_w, mlstm_gate_b, mlstm_norm_g, diff_q_norm_g, diff_k_norm_g,
           diff_lambda, diff_subln_g, hgrn_lb_param, hgrn_norm_g, w_out, norm2_g, w_up, w_down):
    b, t, d = x.shape
    n = b * t
    assert d == D_MODEL and n % ROW_TILE == 0 and t % max(M_CHUNK, H_BLOCK, Q_TILE, K_TILE, PREP_TILE) == 0
    assert K_TILE % Q_TILE == 0

    sm = jax.nn.softmax(hgrn_lb_param.astype(F32), axis=0)
    lower_bounds = jnp.cumsum(sm, axis=0) - sm[0]
    cos, s1, s2 = _rope_tables(t)
    c_m, c_g, c_a = 4 * M_WIDTH, 4 * M_WIDTH + 8, 4 * M_WIDTH + 8 + 3 * A_WIDTH

    x2 = x.reshape(n, d)
    for l in range(DEPTH):
        lambda_init = 0.8 - 0.6 * math.exp(-0.3 * l)
        w = w_in[l]
        wg = jnp.pad(w[:, c_m:c_g], ((0, 0), (0, LANES - 8))).astype(BF16)
        wgt = jnp.pad(w[:, c_m:c_g].T, ((0, 8), (0, 0))).astype(BF16)
        gate_b = mlstm_gate_b[l].astype(F32).reshape(1, 8)
        bg = jnp.pad(gate_b, ((0, 0), (0, LANES - 8)))
        bgt = jnp.broadcast_to(jnp.pad(gate_b.reshape(8, 1), ((0, 8), (0, 0))), (16, LANES))
        um, ua, uh, gc, gt = _inproj(
            x2, norm1_g[l].reshape(1, d), w[:, :c_m].astype(BF16), w[:, c_g:c_a].astype(BF16),
            w[:, c_a:].astype(BF16), wg, wgt, bg, bgt)

        ym = _mlstm(um.reshape(b, t, -1), gc.reshape(b, t, LANES), gt, mlstm_conv_w[l].astype(F32),
                    jnp.tile(mlstm_norm_g[l].astype(F32), 2).reshape(1, LANES))

        qh, kh, vh = _attn_prep(ua.reshape(b, t, -1),
                                jnp.tile(diff_q_norm_g[l].astype(F32), A_WIDTH // HEAD_DIM).reshape(1, A_WIDTH),
                                jnp.tile(diff_k_norm_g[l].astype(F32), A_WIDTH // HEAD_DIM).reshape(1, A_WIDTH),
                                cos, s1, s2)
        lv = diff_lambda[l].astype(F32)
        lam = jnp.exp(jnp.sum(lv[0] * lv[1])) - jnp.exp(jnp.sum(lv[2] * lv[3])) + lambda_init
        ya = _attn(lam.reshape(1), qh, kh, vh, diff_subln_g[l].astype(F32).reshape(1, LANES), 1.0 - lambda_init)

        yh = _hgrn(uh.reshape(b, t, -1), lower_bounds[l].reshape(1, H_WIDTH),
                   jnp.tile(hgrn_norm_g[l].astype(F32), H_WIDTH // HEAD_DIM).reshape(1, H_WIDTH))

        x2 = _post(x2, ym.reshape(n, -1), ya.reshape(n, -1), yh.reshape(n, -1), w_out[l].astype(BF16),
                   norm2_g[l].reshape(1, d), w_up[l].astype(BF16), w_down[l].astype(BF16))
    return x2.reshape(b, t, d)
```

```python
import functools
import math

import jax
import jax.numpy as jnp
from jax import lax
from jax.experimental import pallas as pl
from jax.experimental.pallas import tpu as pltpu

F32 = jnp.float32
BF16 = jnp.bfloat16

D_MODEL = 1024
DEPTH = 2
D_FF = 4 * D_MODEL
EPS = 1e-6
NEG_BIG = -1e30
HEAD_DIM = 64
M_WIDTH = 256
M_CONV = 4
A_WIDTH = 512
A_HEADS = 4
ROT_DIM = 16
ROPE_THETA = 500000.0
H_WIDTH = 256
LANES = 128

VMEM_LIMIT = 56 * 1024 * 1024

ROW_TILE = 512
COL_CHUNK = 512
M_CHUNK = 128
M_BLOCK = 256
H_BLOCK = 256
H_SUB = 16
Q_TILE = 256
K_TILE = 512


def _nt_dot(a, b):
    return lax.dot_general(a, b, (((1,), (1,)), ((), ())), preferred_element_type=F32)


def _dot(a, b):
    return jnp.dot(a, b, preferred_element_type=F32)


def _block_ones(n, group):
    r = lax.broadcasted_iota(jnp.int32, (n, n), 0) // group
    c = lax.broadcasted_iota(jnp.int32, (n, n), 1) // group
    return jnp.where(r == c, 1.0, 0.0).astype(BF16)


def _group_mean(xsq, ones_bd, group):
    return _dot(xsq.astype(BF16), ones_bd) * (1.0 / group)


def _log_sigmoid(x):
    return jnp.minimum(x, 0.0) - jnp.log1p(jnp.exp(-jnp.abs(x)))


def _inproj_kernel(x_ref, g_ref, wm_ref, wq_ref, wk_ref, wv_ref, wh_ref, wf_ref, wg_ref, wgt_ref, bg_ref, bgt_ref,
                   qg_ref, kg_ref, cos_ref, s1_ref, s2_ref,
                   um_ref, q_ref, k_ref, v_ref, uh_ref, uf_ref, gc_ref, gt_ref):
    x = x_ref[...]
    ms = jnp.mean(x * x, axis=-1, keepdims=True)
    h = (x * lax.rsqrt(ms + EPS) * g_ref[...]).astype(BF16)

    def project(w_ref, o_ref):
        width = w_ref.shape[1]
        for c0 in range(0, width, COL_CHUNK):
            sl = slice(c0, min(c0 + COL_CHUNK, width))
            o_ref[:, sl] = _dot(h, w_ref[:, sl]).astype(o_ref.dtype)

    project(wm_ref, um_ref)
    project(wv_ref, v_ref)
    project(wh_ref, uh_ref)
    project(wf_ref, uf_ref)
    gc_ref[...] = _dot(h, wg_ref[...]) + bg_ref[...]
    gt_ref[...] = _nt_dot(wgt_ref[...], h) + bgt_ref[:, 0:1]

    ones_bd = _block_ones(LANES, HEAD_DIM)
    reps = A_WIDTH // LANES
    cos = jnp.concatenate([cos_ref[...]] * reps, axis=1)
    s1 = jnp.concatenate([s1_ref[...]] * reps, axis=1)
    s2 = jnp.concatenate([s2_ref[...]] * reps, axis=1)

    def norm_rope(u, gain):
        ms_g = jnp.concatenate(
            [_group_mean(u[:, i * LANES:(i + 1) * LANES] * u[:, i * LANES:(i + 1) * LANES], ones_bd, HEAD_DIM)
             for i in range(reps)], axis=1)
        un = u * lax.rsqrt(ms_g + EPS) * gain
        return un * cos + pltpu.roll(un, A_WIDTH - ROT_DIM // 2, 1) * s1 + pltpu.roll(un, ROT_DIM // 2, 1) * s2

    q = norm_rope(_dot(h, wq_ref[...]), qg_ref[...]) * (HEAD_DIM ** -0.5 * math.log2(math.e))
    q_ref[...] = q.astype(BF16)
    k_ref[...] = norm_rope(_dot(h, wk_ref[...]), kg_ref[...]).astype(BF16)


def _inproj(x2, g, wm, wq, wk, wv, wh, wf, wg, wgt, bg, bgt, qg, kg, cos, s1, s2):
    n = x2.shape[0]
    t_blocks = cos.shape[0] // ROW_TILE
    const = lambda a: pl.BlockSpec(a.shape, lambda i: (0, 0))
    rows = lambda width: pl.BlockSpec((ROW_TILE, width), lambda i: (i, 0))
    tab = pl.BlockSpec((ROW_TILE, LANES), lambda i: (i % t_blocks, 0))
    out = lambda width, dtype: jax.ShapeDtypeStruct((n, width), dtype)
    return pl.pallas_call(
        _inproj_kernel,
        grid=(n // ROW_TILE,),
        in_specs=[rows(D_MODEL)] + [const(a) for a in (g, wm, wq, wk, wv, wh, wf, wg, wgt, bg, bgt, qg, kg)]
                 + [tab, tab, tab],
        out_specs=[rows(wm.shape[1]), rows(A_WIDTH), rows(A_WIDTH), rows(A_WIDTH), rows(wh.shape[1]),
                   rows(wf.shape[1]), rows(LANES), pl.BlockSpec((16, ROW_TILE), lambda i: (0, i))],
        out_shape=[out(wm.shape[1], BF16), out(A_WIDTH, BF16), out(A_WIDTH, BF16), out(A_WIDTH, BF16),
                   out(wh.shape[1], BF16), out(wf.shape[1], F32), out(LANES, F32),
                   jax.ShapeDtypeStruct((16, n), F32)],
        compiler_params=pltpu.CompilerParams(dimension_semantics=("arbitrary",), vmem_limit_bytes=VMEM_LIMIT),
        name="inproj",
    )(x2, g, wm, wq, wk, wv, wh, wf, wg, wgt, bg, bgt, qg, kg, cos, s1, s2)


def _post_kernel(x_ref, ym_ref, ya_ref, yh_ref, wo_ref, g2_ref, wu_ref, wd_ref, o_ref):
    x1 = x_ref[...]
    x1 = x1 + _dot(ym_ref[...], wo_ref[0:M_WIDTH, :])
    x1 = x1 + _dot(ya_ref[...], wo_ref[M_WIDTH:M_WIDTH + A_WIDTH, :])
    x1 = x1 + _dot(yh_ref[...], wo_ref[M_WIDTH + A_WIDTH:, :])
    ms = jnp.mean(x1 * x1, axis=-1, keepdims=True)
    h2 = (x1 * lax.rsqrt(ms + EPS) * g2_ref[...]).astype(BF16)
    o_ref[...] = x1
    for c in range(D_FF // COL_CHUNK):
        sl = slice(c * COL_CHUNK, (c + 1) * COL_CHUNK)
        a = jnp.maximum(_dot(h2, wu_ref[:, sl]), 0.0)
        o_ref[...] += _dot((a * a).astype(BF16), wd_ref[sl, :])


def _post(x2, ym, ya, yh, wo, g2, wu, wd):
    n = x2.shape[0]
    const = lambda shape: pl.BlockSpec(shape, lambda i: (0, 0), pipeline_mode=pl.Buffered(1))
    rows = lambda width: pl.BlockSpec((ROW_TILE, width), lambda i: (i, 0))
    return pl.pallas_call(
        _post_kernel,
        grid=(n // ROW_TILE,),
        in_specs=[rows(D_MODEL), rows(M_WIDTH), rows(A_WIDTH), rows(H_WIDTH),
                  const(wo.shape), const((1, D_MODEL)), const(wu.shape), const(wd.shape)],
        out_specs=rows(D_MODEL),
        out_shape=jax.ShapeDtypeStruct((n, D_MODEL), F32),
        compiler_params=pltpu.CompilerParams(dimension_semantics=("arbitrary",), vmem_limit_bytes=VMEM_LIMIT),
        name="post",
    )(x2, ym, ya, yh, wo, g2, wu, wd)


def _mlstm_kernel(um_ref, gc_ref, gt_ref, cw_ref, ng_ref, y_ref, tail_ref, gn_ref, m_ref):
    L = M_CHUNK
    R = M_BLOCK
    c = pl.program_id(1)

    @pl.when(c == 0)
    def _():
        tail_ref[...] = jnp.zeros(tail_ref.shape, F32)
        gn_ref[...] = jnp.zeros(gn_ref.shape, F32)
        m_ref[...] = jnp.zeros(m_ref.shape, F32)

    xin = um_ref[0, :, 0:2 * M_WIDTH].astype(F32)
    x3 = xin.reshape(R // 8, 8, 2 * M_WIDTH)
    tail = tail_ref[...].reshape(1, 8, 2 * M_WIDTH)
    sub = lax.broadcasted_iota(jnp.int32, x3.shape, 1)
    conv = None
    for j in range(M_CONV - 1):
        back = M_CONV - 1 - j
        rolled = pltpu.roll(x3, back, 1)
        prev = jnp.concatenate([pltpu.roll(tail, back, 1), rolled[:-1]], axis=0)
        term = jnp.where(sub >= back, rolled, prev) * cw_ref[j:j + 1, :]
        conv = term if conv is None else conv + term
    conv = (conv + x3 * cw_ref[M_CONV - 1:M_CONV, :]).reshape(R, 2 * M_WIDTH)
    tail_ref[...] = xin[R - 8:R, :]
    qk = conv * jax.nn.sigmoid(conv)

    gcol_all = gc_ref[0]
    grow_all = gt_ref[...]
    logf_col_all = _log_sigmoid(gcol_all)
    logf_row_all = _log_sigmoid(grow_all)

    ti = lax.broadcasted_iota(jnp.int32, (L, L), 0)
    si = lax.broadcasted_iota(jnp.int32, (L, L), 1)
    causal = si <= ti
    lane = lax.broadcasted_iota(jnp.int32, (L, LANES), 1)
    left = lane < HEAD_DIM
    ones_bd = _block_ones(LANES, HEAD_DIM)
    gr = lax.broadcasted_iota(jnp.int32, (LANES, 2 * LANES), 0)
    gc_i = lax.broadcasted_iota(jnp.int32, (LANES, 2 * LANES), 1)
    g_rowleft = gr < HEAD_DIM
    g_block = g_rowleft == ((gc_i % LANES) < HEAD_DIM)

    for ci in range(R // L):
        rows = slice(ci * L, (ci + 1) * L)
        gcol, grow = gcol_all[rows], grow_all[:, rows]
        logf_col, logf_row = logf_col_all[rows], logf_row_all[:, rows]
        for p in range(2):
            sl = slice(p * LANES, (p + 1) * LANES)
            q_pair = qk[rows, p * LANES:(p + 1) * LANES]
            k_pair = qk[rows, M_WIDTH + p * LANES:M_WIDTH + (p + 1) * LANES] * (HEAD_DIM ** -0.5)
            v_pair = um_ref[0, rows, 2 * M_WIDTH + p * LANES:2 * M_WIDTH + (p + 1) * LANES]
            o_pair = um_ref[0, rows, 3 * M_WIDTH + p * LANES:3 * M_WIDTH + (p + 1) * LANES].astype(F32)
            v1 = jnp.concatenate([v_pair, jnp.ones((L, LANES), v_pair.dtype)], axis=1)
            k_bf = k_pair.astype(BF16)
            intra, scale, einv, wcol, decay = [], [], [], [], []
            for hh in range(2):
                h = 2 * p + hh
                m_prev = m_ref[h:h + 1, 0:1]
                i_row = grow[h:h + 1, :]
                i_col = gcol[:, h:h + 1]
                lf_row = logf_row[4 + h:5 + h, :]
                lf_col = logf_col[:, 4 + h:5 + h]
                b_col = jnp.sum(jnp.where(causal, lf_row, 0.0), axis=1, keepdims=True)
                b_row = jnp.sum(jnp.where(ti <= si, lf_col, 0.0), axis=0, keepdims=True)
                a_row = i_row - b_row
                a_col = i_col - b_col
                dm = jnp.where(causal, a_row, NEG_BIG)
                m_col = jnp.maximum(jnp.max(dm, axis=1, keepdims=True), m_prev)
                e = jnp.exp(dm - m_col)
                qm = jnp.where(left if hh == 0 else jnp.logical_not(left), q_pair, 0.0).astype(BF16)
                s = _nt_dot(qm, k_bf) * e
                intra.append(_dot(s.astype(BF16), v1))
                m_last = m_col[L - 1:L, :]
                scale.append(jnp.exp(m_prev - m_col))
                einv.append(jnp.exp(-(b_col + m_col)))
                wcol.append(jnp.exp(a_col - m_last))
                decay.append(jnp.exp(m_prev - m_last))
                m_ref[h:h + 1, :] = jnp.broadcast_to(b_col[L - 1:L, :] + m_last, (1, LANES))

            gn = gn_ref[p]
            inter = _dot(q_pair.astype(BF16), gn.astype(BF16))
            sc = jnp.where(left, scale[0], scale[1])
            num = jnp.where(left, intra[0][:, :LANES], intra[1][:, :LANES]) + sc * inter[:, :LANES]
            den = jnp.where(left, intra[0][:, LANES:], intra[1][:, LANES:]) + sc * inter[:, LANES:]
            hv = num / jnp.maximum(jnp.abs(den), jnp.where(left, einv[0], einv[1]))

            kw = k_pair * jnp.where(left, wcol[0], wcol[1])
            upd = _dot(kw.T.astype(BF16), v1)
            gn_ref[p] = jnp.where(g_rowleft, decay[0], decay[1]) * gn + jnp.where(g_block, upd, 0.0)

            ms = _group_mean(hv * hv, ones_bd, HEAD_DIM)
            y = jax.nn.sigmoid(o_pair) * (hv * lax.rsqrt(ms + EPS) * ng_ref[...])
            y_ref[0, rows, sl] = y.astype(y_ref.dtype)


def _mlstm(um, gc, gt, conv_w, ng):
    b, t, _ = um.shape
    nb = t // M_BLOCK
    return pl.pallas_call(
        _mlstm_kernel,
        grid=(b, nb),
        in_specs=[pl.BlockSpec((1, M_BLOCK, 4 * M_WIDTH), lambda i, c: (i, c, 0)),
                  pl.BlockSpec((1, M_BLOCK, LANES), lambda i, c: (i, c, 0)),
                  pl.BlockSpec((16, M_BLOCK), lambda i, c: (0, i * nb + c)),
                  pl.BlockSpec((M_CONV, 2 * M_WIDTH), lambda i, c: (0, 0)),
                  pl.BlockSpec((1, LANES), lambda i, c: (0, 0))],
        out_specs=pl.BlockSpec((1, M_BLOCK, M_WIDTH), lambda i, c: (i, c, 0)),
        out_shape=jax.ShapeDtypeStruct((b, t, M_WIDTH), BF16),
        scratch_shapes=[pltpu.VMEM((8, 2 * M_WIDTH), F32),
                        pltpu.VMEM((2, LANES, 2 * LANES), F32),
                        pltpu.VMEM((8, LANES), F32)],
        compiler_params=pltpu.CompilerParams(dimension_semantics=("arbitrary", "arbitrary"),
                                             vmem_limit_bytes=VMEM_LIMIT),
        name="mlstm",
    )(um, gc, gt, conv_w, ng)


def _hgrn_kernel(uh_ref, uf_ref, lb_ref, ng_ref, y_ref, w_ref, st_ref):
    R = H_BLOCK
    NS = R // H_SUB
    c = pl.program_id(1)

    @pl.when(c == 0)
    def _():
        st_ref[...] = jnp.zeros(st_ref.shape, F32)

    lb = lb_ref[...]
    fp = uf_ref[0]
    hq = uh_ref[0, :, 0:H_WIDTH].astype(F32)
    q = hq * jax.nn.sigmoid(hq)
    logf = jnp.log(lb + (1.0 - lb) * jax.nn.sigmoid(fp))
    k = (1.0 - lb) * jax.nn.sigmoid(-fp)
    v = uh_ref[0, :, H_WIDTH:2 * H_WIDTH].astype(F32)

    G = R // H_SUB
    HALF = H_SUB // 2

    def halves(x):
        x4 = x.reshape(G, 2, HALF, H_WIDTH)
        return x4[:, 0], x4[:, 1]

    def whole(lo, hi):
        return jnp.stack([lo, hi], axis=1).reshape(R, H_WIDTH)

    sub = lax.broadcasted_iota(jnp.int32, (G, HALF, H_WIDTH), 1)

    def cumsum_half(x):
        shift = 1
        while shift < HALF:
            x = x + jnp.where(sub >= shift, pltpu.roll(x, shift, 1), 0.0)
            shift *= 2
        return x

    lf_lo, lf_hi = halves(logf)
    bc_lo = cumsum_half(lf_lo)
    bc_hi = cumsum_half(lf_hi) + bc_lo[:, HALF - 1:HALF, :]
    bc = whole(bc_lo, bc_hi)
    q_lo, q_hi = halves(q)
    k_lo, k_hi = halves(k)
    v_lo, v_hi = halves(v)

    ones_bd = _block_ones(LANES, HEAD_DIM)

    def weights(qq, bq, bs, ks, valid):
        dlt = bq - bs
        if valid is not None:
            dlt = jnp.where(valid, dlt, NEG_BIG)
        return (qq * ks * jnp.exp(dlt)).reshape(G * HALF, H_WIDTH)

    o_lo = jnp.zeros((G, HALF, H_WIDTH), F32)
    o_hi = jnp.zeros((G, HALF, H_WIDTH), F32)
    for b in range(HALF):
        if b == 0:
            valid = None
            bs_l, ks_l, vs_l = bc_lo, k_lo, v_lo
            bs_h, ks_h, vs_h = bc_hi, k_hi, v_hi
        else:
            valid = sub >= b
            bs_l, ks_l, vs_l = (pltpu.roll(x, b, 1) for x in (bc_lo, k_lo, v_lo))
            bs_h, ks_h, vs_h = (jnp.where(valid, pltpu.roll(x, b, 1), y)
                                for x, y in ((bc_hi, bs_l), (k_hi, ks_l), (v_hi, vs_l)))
        pr = jnp.concatenate([weights(q_lo, bc_lo, bs_l, ks_l, valid),
                              weights(q_hi, bc_hi, bs_h, ks_h, None),
                              weights(q_hi, bc_hi, bs_l, ks_l, valid)], axis=0).astype(BF16)
        red = jnp.concatenate([_dot(pr[:, 0:LANES], ones_bd), _dot(pr[:, LANES:], ones_bd)], axis=1)
        red = red.reshape(3, G, HALF, H_WIDTH)
        o_lo = o_lo + red[0] * vs_l
        o_hi = o_hi + red[1] * vs_h + red[2] * vs_l
    o = whole(o_lo, o_hi)

    bl_rows = jnp.concatenate(
        [jnp.broadcast_to(bc[(j + 1) * H_SUB - 1:(j + 1) * H_SUB, :], (H_SUB, H_WIDTH)) for j in range(NS)], axis=0)
    qe = q * jnp.exp(bc)
    ke = k * jnp.exp(bl_rows - bc)
    tcol = lax.broadcasted_iota(jnp.int32, (LANES, R), 1) // H_SUB
    trow = lax.broadcasted_iota(jnp.int32, (R, LANES), 0) // H_SUB
    sr = lax.broadcasted_iota(jnp.int32, (LANES, LANES), 0) < HEAD_DIM
    s_block = sr == (lax.broadcasted_iota(jnp.int32, (LANES, LANES), 1) < HEAD_DIM)
    outs = []
    for p in range(2):
        sl = slice(p * LANES, (p + 1) * LANES)
        v_t = v[:, sl].T
        v_stack = jnp.concatenate([jnp.where(tcol == j, v_t, 0.0) for j in range(NS)], axis=0).astype(BF16)
        upd = _dot(v_stack, ke[:, sl].astype(BF16))
        st = st_ref[p]
        for j in range(NS):
            w_ref[:, j * LANES:(j + 1) * LANES] = st
            a_row = jnp.exp(bc[(j + 1) * H_SUB - 1:(j + 1) * H_SUB, sl])
            st = st * a_row + jnp.where(s_block, upd[j * LANES:(j + 1) * LANES, :], 0.0)
        st_ref[p] = st
        q_exp = jnp.concatenate([jnp.where(trow == j, qe[:, sl], 0.0) for j in range(NS)], axis=1).astype(BF16)
        outs.append(_nt_dot(q_exp, w_ref[...].astype(BF16)))
    o = o + jnp.concatenate(outs, axis=1)

    ms = jnp.concatenate([_group_mean(o[:, 0:LANES] * o[:, 0:LANES], ones_bd, HEAD_DIM),
                          _group_mean(o[:, LANES:] * o[:, LANES:], ones_bd, HEAD_DIM)], axis=1)
    hg = uh_ref[0, :, 2 * H_WIDTH:3 * H_WIDTH].astype(F32)
    y = (o * lax.rsqrt(ms + EPS) * ng_ref[...]) * (hg * jax.nn.sigmoid(hg))
    y_ref[0] = y.astype(y_ref.dtype)


def _hgrn(uh, uf, lb, ng):
    b, t, _ = uh.shape
    ns = H_BLOCK // H_SUB
    return pl.pallas_call(
        _hgrn_kernel,
        grid=(b, t // H_BLOCK),
        in_specs=[pl.BlockSpec((1, H_BLOCK, 3 * H_WIDTH), lambda i, c: (i, c, 0)),
                  pl.BlockSpec((1, H_BLOCK, H_WIDTH), lambda i, c: (i, c, 0)),
                  pl.BlockSpec((1, H_WIDTH), lambda i, c: (0, 0)),
                  pl.BlockSpec((1, H_WIDTH), lambda i, c: (0, 0))],
        out_specs=pl.BlockSpec((1, H_BLOCK, H_WIDTH), lambda i, c: (i, c, 0)),
        out_shape=jax.ShapeDtypeStruct((b, t, H_WIDTH), BF16),
        scratch_shapes=[pltpu.VMEM((LANES, ns * LANES), F32),
                        pltpu.VMEM((2, LANES, LANES), F32)],
        compiler_params=pltpu.CompilerParams(dimension_semantics=("arbitrary", "arbitrary"),
                                             vmem_limit_bytes=VMEM_LIMIT),
        name="hgrn2",
    )(uh, uf, lb, ng)


def _attn_kernel(tab_ref, lam_ref, q_ref, k_ref, v_ref, sg_ref, y_ref,
                 qs_ref, d_ref, s0_ref, s1_ref, p0_ref, p1_ref, a0_ref, a1_ref, m_ref, acc_ref, *,
                 out_scale, n_steps):
    n_q = q_ref.shape[1] // Q_TILE
    lane = lax.broadcasted_iota(jnp.int32, (Q_TILE, LANES), 1)
    for i in range(n_q):
        q = q_ref[0, i * Q_TILE:(i + 1) * Q_TILE, :]
        zero = jnp.zeros_like(q)
        qs_ref[i, 0:Q_TILE, :] = jnp.where(lane < HEAD_DIM, q, zero)
        qs_ref[i, Q_TILE:, :] = jnp.where(lane >= HEAD_DIM, q, zero)
    r = lax.broadcasted_iota(jnp.int32, (2 * Q_TILE, K_TILE), 0)
    d_ref[...] = jnp.where(r >= Q_TILE, r - Q_TILE, r) - lax.broadcasted_iota(jnp.int32, (2 * Q_TILE, K_TILE), 1)
    ones = jnp.ones((K_TILE, LANES), BF16)

    def kstart(t):
        return pl.multiple_of(tab_ref[1, t] * K_TILE, K_TILE)

    def scores(t, s_ref):
        s_ref[...] = _nt_dot(qs_ref[tab_ref[0, t]], k_ref[0, pl.ds(kstart(t), K_TILE), :])

    def numerators(t, s_ref, p_ref, a_ref):
        visible = d_ref[...] >= kstart(t) - tab_ref[0, t] * Q_TILE
        s = jnp.where(visible, s_ref[...], NEG_BIG)
        m_old = m_ref[...]
        m_new = jnp.maximum(m_old, jnp.max(s, axis=1, keepdims=True))
        p_ref[...] = jnp.exp2(s - m_new).astype(BF16)
        a_ref[...] = jnp.exp2(m_old - m_new)
        m_ref[...] = jnp.where(tab_ref[2, t] == 1, NEG_BIG, m_new)

    def values(t, p_ref, a_ref):
        v1 = jnp.concatenate([v_ref[0, pl.ds(kstart(t), K_TILE), :], ones], axis=1)
        acc_ref[...] = a_ref[...] * acc_ref[...] + _dot(p_ref[...], v1)

        @pl.when(tab_ref[2, t] == 1)
        def _():
            acc = acc_ref[...]
            o = (acc[:Q_TILE, :LANES] / acc[:Q_TILE, LANES:]
                 - lam_ref[0] * (acc[Q_TILE:, :LANES] / acc[Q_TILE:, LANES:]))
            ms = jnp.mean(o * o, axis=-1, keepdims=True)
            y = (o * lax.rsqrt(ms + EPS) * sg_ref[...]) * out_scale
            y_ref[0, pl.ds(pl.multiple_of(tab_ref[0, t] * Q_TILE, Q_TILE), Q_TILE), :] = y.astype(y_ref.dtype)

    m_ref[...] = jnp.full(m_ref.shape, NEG_BIG, F32)
    acc_ref[...] = jnp.zeros(acc_ref.shape, F32)
    scores(0, s0_ref)
    scores(1, s1_ref)
    numerators(0, s0_ref, p0_ref, a0_ref)

    def two_steps(i, carry):
        t = 2 * i
        scores(t + 2, s0_ref)
        numerators(t + 1, s1_ref, p1_ref, a1_ref)
        values(t, p0_ref, a0_ref)
        scores(t + 3, s1_ref)
        numerators(t + 2, s0_ref, p0_ref, a0_ref)
        values(t + 1, p1_ref, a1_ref)
        return carry

    lax.fori_loop(0, n_steps // 2, two_steps, 0)


def _attn_schedule(t):
    cols = []
    for qi in range(t // Q_TILE):
        n_kv = (qi * Q_TILE) // K_TILE + 1
        cols += [(qi, kj, int(kj == n_kv - 1)) for kj in range(n_kv)]
    assert len(cols) % 2 == 0
    n_steps = len(cols)
    cols += [(0, 0, 0)] * 2
    return n_steps, jnp.asarray(cols, jnp.int32).T


def _attn(lam, qh, kh, vh, sg, out_scale):
    b, t, _ = qh.shape
    n_steps, tab = _attn_schedule(t)
    seq = pl.BlockSpec((1, t, LANES), lambda i, h: (i, 0, h))
    smem = pl.BlockSpec(memory_space=pltpu.SMEM)
    rows = 2 * Q_TILE
    return pl.pallas_call(
        functools.partial(_attn_kernel, out_scale=out_scale, n_steps=n_steps),
        grid=(b, A_HEADS),
        in_specs=[smem, smem, seq, seq, seq, pl.BlockSpec((1, LANES), lambda i, h: (0, 0))],
        out_specs=seq,
        out_shape=jax.ShapeDtypeStruct((b, t, A_WIDTH), BF16),
        scratch_shapes=[pltpu.VMEM((t // Q_TILE, rows, LANES), BF16),
                        pltpu.VMEM((rows, K_TILE), jnp.int32),
                        pltpu.VMEM((rows, K_TILE), F32), pltpu.VMEM((rows, K_TILE), F32),
                        pltpu.VMEM((rows, K_TILE), BF16), pltpu.VMEM((rows, K_TILE), BF16),
                        pltpu.VMEM((rows, 1), F32), pltpu.VMEM((rows, 1), F32),
                        pltpu.VMEM((rows, 1), F32),
                        pltpu.VMEM((rows, 2 * LANES), F32)],
        compiler_params=pltpu.CompilerParams(dimension_semantics=("arbitrary", "arbitrary"),
                                             vmem_limit_bytes=VMEM_LIMIT),
        name="diff_attn",
    )(tab, lam, qh, kh, vh, sg)


def _rope_tables(t):
    half = ROT_DIM // 2
    inv = ROPE_THETA ** (-jnp.arange(half, dtype=F32) / half)
    ang = jnp.arange(t, dtype=F32)[:, None] * inv[None, :]
    cos, sin = jnp.cos(ang), jnp.sin(ang)
    pad = jnp.zeros((t, HEAD_DIM - ROT_DIM), F32)
    zero = jnp.zeros((t, half), F32)
    c = jnp.concatenate([cos, cos, pad + 1.0], axis=1)
    s1 = jnp.concatenate([-sin, zero, pad], axis=1)
    s2 = jnp.concatenate([zero, sin, pad], axis=1)
    tile = lambda a: jnp.concatenate([a, a], axis=1)
    return tile(c), tile(s1), tile(s2)


def kernel(x, norm1_g, w_in, mlstm_conv_w, mlstm_gate_b, mlstm_norm_g, diff_q_norm_g, diff_k_norm_g,
           diff_lambda, diff_subln_g, hgrn_lb_param, hgrn_norm_g, w_out, norm2_g, w_up, w_down):
    b, t, d = x.shape
    n = b * t
    assert d == D_MODEL and t % max(ROW_TILE, M_BLOCK, H_BLOCK, Q_TILE, K_TILE) == 0
    assert K_TILE % Q_TILE == 0

    sm = jax.nn.softmax(hgrn_lb_param.astype(F32), axis=0)
    lower_bounds = jnp.cumsum(sm, axis=0) - sm[0]
    cos, s1, s2 = _rope_tables(t)
    c_g = 4 * M_WIDTH
    c_a = c_g + 8
    c_h = c_a + 3 * A_WIDTH
    head_tile = lambda g, width: jnp.tile(g.astype(F32), width // HEAD_DIM).reshape(1, width)

    x2 = x.reshape(n, d)
    for l in range(DEPTH):
        lambda_init = 0.8 - 0.6 * math.exp(-0.3 * l)
        w = w_in[l]
        cols = lambda lo, hi: w[:, lo:hi].astype(BF16)
        wg = jnp.pad(w[:, c_g:c_a], ((0, 0), (0, LANES - 8))).astype(BF16)
        wgt = jnp.pad(w[:, c_g:c_a].T, ((0, 8), (0, 0))).astype(BF16)
        gate_b = mlstm_gate_b[l].astype(F32).reshape(1, 8)
        bg = jnp.pad(gate_b, ((0, 0), (0, LANES - 8)))
        bgt = jnp.broadcast_to(jnp.pad(gate_b.reshape(8, 1), ((0, 8), (0, 0))), (16, LANES))
        wh = jnp.concatenate([w[:, c_h:c_h + H_WIDTH], w[:, c_h + 2 * H_WIDTH:]], axis=1).astype(BF16)
        um, qh, kh, vh, uh, uf, gc, gt = _inproj(
            x2, norm1_g[l].reshape(1, d), cols(0, c_g),
            cols(c_a, c_a + A_WIDTH), cols(c_a + A_WIDTH, c_a + 2 * A_WIDTH), cols(c_a + 2 * A_WIDTH, c_h),
            wh, cols(c_h + H_WIDTH, c_h + 2 * H_WIDTH), wg, wgt, bg, bgt,
            head_tile(diff_q_norm_g[l], A_WIDTH), head_tile(diff_k_norm_g[l], A_WIDTH), cos, s1, s2)

        ym = _mlstm(um.reshape(b, t, -1), gc.reshape(b, t, LANES), gt, mlstm_conv_w[l].astype(F32),
                    head_tile(mlstm_norm_g[l], LANES))

        lv = diff_lambda[l].astype(F32)
        lam = jnp.exp(jnp.sum(lv[0] * lv[1])) - jnp.exp(jnp.sum(lv[2] * lv[3])) + lambda_init
        ya = _attn(lam.reshape(1), qh.reshape(b, t, -1), kh.reshape(b, t, -1), vh.reshape(b, t, -1),
                   diff_subln_g[l].astype(F32).reshape(1, LANES), 1.0 - lambda_init)

        yh = _hgrn(uh.reshape(b, t, -1), uf.reshape(b, t, -1), lower_bounds[l].reshape(1, H_WIDTH),
                   head_tile(hgrn_norm_g[l], H_WIDTH))

        x2 = _post(x2, ym.reshape(n, -1), ya.reshape(n, -1), yh.reshape(n, -1), w_out[l].astype(BF16),
                   norm2_g[l].reshape(1, d), w_up[l].astype(BF16), w_down[l].astype(BF16))
    return x2.reshape(b, t, d)
```

```python
import functools
import math

import jax
import jax.numpy as jnp
from jax import lax
from jax.experimental import pallas as pl
from jax.experimental.pallas import tpu as pltpu

F32 = jnp.float32
BF16 = jnp.bfloat16

D_MODEL = 1024
DEPTH = 2
D_FF = 4 * D_MODEL
EPS = 1e-6
NEG_BIG = -1e30
HEAD_DIM = 64
M_WIDTH = 256
M_CONV = 4
A_WIDTH = 512
A_HEADS = 4
ROT_DIM = 16
ROPE_THETA = 500000.0
H_WIDTH = 256
LANES = 128

VMEM_LIMIT = 56 * 1024 * 1024

ROW_TILE = 512
COL_CHUNK = 512
M_CHUNK = 128
M_BLOCK = 512
H_BLOCK = 256
H_SUB = 16
Q_TILE = 256
K_TILE = 512


def _nt_dot(a, b):
    return lax.dot_general(a, b, (((1,), (1,)), ((), ())), preferred_element_type=F32)


def _dot(a, b):
    return jnp.dot(a, b, preferred_element_type=F32)


def _block_ones(n, group):
    r = lax.broadcasted_iota(jnp.int32, (n, n), 0) // group
    c = lax.broadcasted_iota(jnp.int32, (n, n), 1) // group
    return jnp.where(r == c, 1.0, 0.0).astype(BF16)


def _group_mean(xsq, ones_bd, group):
    return _dot(xsq.astype(BF16), ones_bd) * (1.0 / group)


def _log_sigmoid(x):
    return jnp.minimum(x, 0.0) - jnp.log1p(jnp.exp(-jnp.abs(x)))


def _inproj_kernel(x_ref, g_ref, wm_ref, wq_ref, wk_ref, wv_ref, wh_ref, wf_ref, wg_ref, wgt_ref, bg_ref, bgt_ref,
                   qg_ref, kg_ref, cos_ref, s1_ref, s2_ref,
                   um_ref, q_ref, k_ref, v_ref, uh_ref, uf_ref, gc_ref, gt_ref):
    x = x_ref[...]
    ms = jnp.mean(x * x, axis=-1, keepdims=True)
    h = (x * lax.rsqrt(ms + EPS) * g_ref[...]).astype(BF16)

    def project(w_ref, o_ref):
        width = w_ref.shape[1]
        for c0 in range(0, width, COL_CHUNK):
            sl = slice(c0, min(c0 + COL_CHUNK, width))
            o_ref[:, sl] = _dot(h, w_ref[:, sl]).astype(o_ref.dtype)

    project(wm_ref, um_ref)
    project(wv_ref, v_ref)
    project(wh_ref, uh_ref)
    project(wf_ref, uf_ref)
    gc_ref[...] = _dot(h, wg_ref[...]) + bg_ref[...]
    gt_ref[...] = _nt_dot(wgt_ref[...], h) + bgt_ref[:, 0:1]

    ones_bd = _block_ones(LANES, HEAD_DIM)
    reps = A_WIDTH // LANES
    cos = jnp.concatenate([cos_ref[...]] * reps, axis=1)
    s1 = jnp.concatenate([s1_ref[...]] * reps, axis=1)
    s2 = jnp.concatenate([s2_ref[...]] * reps, axis=1)

    def norm_rope(u, gain):
        ms_g = jnp.concatenate(
            [_group_mean(u[:, i * LANES:(i + 1) * LANES] * u[:, i * LANES:(i + 1) * LANES], ones_bd, HEAD_DIM)
             for i in range(reps)], axis=1)
        un = u * lax.rsqrt(ms_g + EPS) * gain
        return un * cos + pltpu.roll(un, A_WIDTH - ROT_DIM // 2, 1) * s1 + pltpu.roll(un, ROT_DIM // 2, 1) * s2

    q = norm_rope(_dot(h, wq_ref[...]), qg_ref[...]) * (HEAD_DIM ** -0.5 * math.log2(math.e))
    q_ref[...] = q.astype(BF16)
    k_ref[...] = norm_rope(_dot(h, wk_ref[...]), kg_ref[...]).astype(BF16)


def _inproj(x2, g, wm, wq, wk, wv, wh, wf, wg, wgt, bg, bgt, qg, kg, cos, s1, s2):
    n = x2.shape[0]
    t_blocks = cos.shape[0] // ROW_TILE
    const = lambda a: pl.BlockSpec(a.shape, lambda i: (0, 0))
    rows = lambda width: pl.BlockSpec((ROW_TILE, width), lambda i: (i, 0))
    tab = pl.BlockSpec((ROW_TILE, LANES), lambda i: (i % t_blocks, 0))
    out = lambda width, dtype: jax.ShapeDtypeStruct((n, width), dtype)
    return pl.pallas_call(
        _inproj_kernel,
        grid=(n // ROW_TILE,),
        in_specs=[rows(D_MODEL)] + [const(a) for a in (g, wm, wq, wk, wv, wh, wf, wg, wgt, bg, bgt, qg, kg)]
                 + [tab, tab, tab],
        out_specs=[rows(wm.shape[1]), rows(A_WIDTH), rows(A_WIDTH), rows(A_WIDTH), rows(wh.shape[1]),
                   rows(wf.shape[1]), rows(LANES), pl.BlockSpec((16, ROW_TILE), lambda i: (0, i))],
        out_shape=[out(wm.shape[1], BF16), out(A_WIDTH, BF16), out(A_WIDTH, BF16), out(A_WIDTH, BF16),
                   out(wh.shape[1], BF16), out(wf.shape[1], F32), out(LANES, F32),
                   jax.ShapeDtypeStruct((16, n), F32)],
        compiler_params=pltpu.CompilerParams(dimension_semantics=("arbitrary",), vmem_limit_bytes=VMEM_LIMIT),
        name="inproj",
    )(x2, g, wm, wq, wk, wv, wh, wf, wg, wgt, bg, bgt, qg, kg, cos, s1, s2)


def _post_kernel(x_ref, ym_ref, ya_ref, yh_ref, wo_ref, g2_ref, wu_ref, wd_ref, o_ref):
    x1 = x_ref[...]
    x1 = x1 + _dot(ym_ref[...], wo_ref[0:M_WIDTH, :])
    x1 = x1 + _dot(ya_ref[...], wo_ref[M_WIDTH:M_WIDTH + A_WIDTH, :])
    x1 = x1 + _dot(yh_ref[...], wo_ref[M_WIDTH + A_WIDTH:, :])
    ms = jnp.mean(x1 * x1, axis=-1, keepdims=True)
    h2 = (x1 * lax.rsqrt(ms + EPS) * g2_ref[...]).astype(BF16)
    o_ref[...] = x1
    for c in range(D_FF // COL_CHUNK):
        sl = slice(c * COL_CHUNK, (c + 1) * COL_CHUNK)
        a = jnp.maximum(_dot(h2, wu_ref[:, sl]), 0.0)
        o_ref[...] += _dot((a * a).astype(BF16), wd_ref[sl, :])


def _post(x2, ym, ya, yh, wo, g2, wu, wd):
    n = x2.shape[0]
    const = lambda shape: pl.BlockSpec(shape, lambda i: (0, 0), pipeline_mode=pl.Buffered(1))
    rows = lambda width: pl.BlockSpec((ROW_TILE, width), lambda i: (i, 0))
    return pl.pallas_call(
        _post_kernel,
        grid=(n // ROW_TILE,),
        in_specs=[rows(D_MODEL), rows(M_WIDTH), rows(A_WIDTH), rows(H_WIDTH),
                  const(wo.shape), const((1, D_MODEL)), const(wu.shape), const(wd.shape)],
        out_specs=rows(D_MODEL),
        out_shape=jax.ShapeDtypeStruct((n, D_MODEL), F32),
        compiler_params=pltpu.CompilerParams(dimension_semantics=("arbitrary",), vmem_limit_bytes=VMEM_LIMIT),
        name="post",
    )(x2, ym, ya, yh, wo, g2, wu, wd)


def _mlstm_kernel(um_ref, gc_ref, gt_ref, cw_ref, ng_ref, y_ref, tail_ref, gn_ref, m_ref):
    L = M_CHUNK
    R = M_BLOCK
    c = pl.program_id(1)

    @pl.when(c == 0)
    def _():
        tail_ref[...] = jnp.zeros(tail_ref.shape, F32)
        gn_ref[...] = jnp.zeros(gn_ref.shape, F32)
        m_ref[...] = jnp.zeros(m_ref.shape, F32)

    xin = um_ref[0, :, 0:2 * M_WIDTH].astype(F32)
    x3 = xin.reshape(R // 8, 8, 2 * M_WIDTH)
    tail = tail_ref[...].reshape(1, 8, 2 * M_WIDTH)
    sub = lax.broadcasted_iota(jnp.int32, x3.shape, 1)
    conv = None
    for j in range(M_CONV - 1):
        back = M_CONV - 1 - j
        rolled = pltpu.roll(x3, back, 1)
        prev = jnp.concatenate([pltpu.roll(tail, back, 1), rolled[:-1]], axis=0)
        term = jnp.where(sub >= back, rolled, prev) * cw_ref[j:j + 1, :]
        conv = term if conv is None else conv + term
    conv = (conv + x3 * cw_ref[M_CONV - 1:M_CONV, :]).reshape(R, 2 * M_WIDTH)
    tail_ref[...] = xin[R - 8:R, :]
    qk = conv * jax.nn.sigmoid(conv)

    gcol_all = gc_ref[0]
    grow_all = gt_ref[...]
    logf_col_all = _log_sigmoid(gcol_all)
    logf_row_all = _log_sigmoid(grow_all)

    ti = lax.broadcasted_iota(jnp.int32, (L, L), 0)
    si = lax.broadcasted_iota(jnp.int32, (L, L), 1)
    causal = si <= ti
    lane = lax.broadcasted_iota(jnp.int32, (L, LANES), 1)
    left = lane < HEAD_DIM
    ones_bd = _block_ones(LANES, HEAD_DIM)
    gr = lax.broadcasted_iota(jnp.int32, (LANES, 2 * LANES), 0)
    gc_i = lax.broadcasted_iota(jnp.int32, (LANES, 2 * LANES), 1)
    g_rowleft = gr < HEAD_DIM
    g_block = g_rowleft == ((gc_i % LANES) < HEAD_DIM)

    for ci in range(R // L):
        rows = slice(ci * L, (ci + 1) * L)
        gcol, grow = gcol_all[rows], grow_all[:, rows]
        logf_col, logf_row = logf_col_all[rows], logf_row_all[:, rows]
        for p in range(2):
            sl = slice(p * LANES, (p + 1) * LANES)
            q_pair = qk[rows, p * LANES:(p + 1) * LANES]
            k_pair = qk[rows, M_WIDTH + p * LANES:M_WIDTH + (p + 1) * LANES] * (HEAD_DIM ** -0.5)
            v_pair = um_ref[0, rows, 2 * M_WIDTH + p * LANES:2 * M_WIDTH + (p + 1) * LANES]
            o_pair = um_ref[0, rows, 3 * M_WIDTH + p * LANES:3 * M_WIDTH + (p + 1) * LANES].astype(F32)
            v1 = jnp.concatenate([v_pair, jnp.ones((L, LANES), v_pair.dtype)], axis=1)
            k_bf = k_pair.astype(BF16)
            intra, scale, einv, wcol, decay = [], [], [], [], []
            for hh in range(2):
                h = 2 * p + hh
                m_prev = m_ref[h:h + 1, 0:1]
                i_row = grow[h:h + 1, :]
                i_col = gcol[:, h:h + 1]
                lf_row = logf_row[4 + h:5 + h, :]
                lf_col = logf_col[:, 4 + h:5 + h]
                b_col = jnp.sum(jnp.where(causal, lf_row, 0.0), axis=1, keepdims=True)
                b_row = jnp.sum(jnp.where(ti <= si, lf_col, 0.0), axis=0, keepdims=True)
                a_row = i_row - b_row
                a_col = i_col - b_col
                dm = jnp.where(causal, a_row, NEG_BIG)
                m_col = jnp.maximum(jnp.max(dm, axis=1, keepdims=True), m_prev)
                e = jnp.exp(dm - m_col)
                qm = jnp.where(left if hh == 0 else jnp.logical_not(left), q_pair, 0.0).astype(BF16)
                s = _nt_dot(qm, k_bf) * e
                intra.append(_dot(s.astype(BF16), v1))
                m_last = m_col[L - 1:L, :]
                scale.append(jnp.exp(m_prev - m_col))
                einv.append(jnp.exp(-(b_col + m_col)))
                wcol.append(jnp.exp(a_col - m_last))
                decay.append(jnp.exp(m_prev - m_last))
                m_ref[h:h + 1, :] = jnp.broadcast_to(b_col[L - 1:L, :] + m_last, (1, LANES))

            gn = gn_ref[p]
            inter = _dot(q_pair.astype(BF16), gn.astype(BF16))
            sc = jnp.where(left, scale[0], scale[1])
            num = jnp.where(left, intra[0][:, :LANES], intra[1][:, :LANES]) + sc * inter[:, :LANES]
            den = jnp.where(left, intra[0][:, LANES:], intra[1][:, LANES:]) + sc * inter[:, LANES:]
            hv = num / jnp.maximum(jnp.abs(den), jnp.where(left, einv[0], einv[1]))

            kw = k_pair * jnp.where(left, wcol[0], wcol[1])
            upd = _dot(kw.T.astype(BF16), v1)
            gn_ref[p] = jnp.where(g_rowleft, decay[0], decay[1]) * gn + jnp.where(g_block, upd, 0.0)

            ms = _group_mean(hv * hv, ones_bd, HEAD_DIM)
            y = jax.nn.sigmoid(o_pair) * (hv * lax.rsqrt(ms + EPS) * ng_ref[...])
            y_ref[0, rows, sl] = y.astype(y_ref.dtype)


def _mlstm(um, gc, gt, conv_w, ng):
    b, t, _ = um.shape
    nb = t // M_BLOCK
    return pl.pallas_call(
        _mlstm_kernel,
        grid=(b, nb),
        in_specs=[pl.BlockSpec((1, M_BLOCK, 4 * M_WIDTH), lambda i, c: (i, c, 0)),
                  pl.BlockSpec((1, M_BLOCK, LANES), lambda i, c: (i, c, 0)),
                  pl.BlockSpec((16, M_BLOCK), lambda i, c: (0, i * nb + c)),
                  pl.BlockSpec((M_CONV, 2 * M_WIDTH), lambda i, c: (0, 0)),
                  pl.BlockSpec((1, LANES), lambda i, c: (0, 0))],
        out_specs=pl.BlockSpec((1, M_BLOCK, M_WIDTH), lambda i, c: (i, c, 0)),
        out_shape=jax.ShapeDtypeStruct((b, t, M_WIDTH), BF16),
        scratch_shapes=[pltpu.VMEM((8, 2 * M_WIDTH), F32),
                        pltpu.VMEM((2, LANES, 2 * LANES), F32),
                        pltpu.VMEM((8, LANES), F32)],
        compiler_params=pltpu.CompilerParams(dimension_semantics=("arbitrary", "arbitrary"),
                                             vmem_limit_bytes=VMEM_LIMIT),
        name="mlstm",
    )(um, gc, gt, conv_w, ng)


def _hgrn_kernel(uh_ref, uf_ref, lb_ref, ng_ref, y_ref, w_ref, st_ref):
    R = H_BLOCK
    NS = R // H_SUB
    c = pl.program_id(1)

    @pl.when(c == 0)
    def _():
        st_ref[...] = jnp.zeros(st_ref.shape, F32)

    lb = lb_ref[...]
    fp = uf_ref[0]
    hq = uh_ref[0, :, 0:H_WIDTH].astype(F32)
    q = hq * jax.nn.sigmoid(hq)
    logf = jnp.log(lb + (1.0 - lb) * jax.nn.sigmoid(fp))
    k = (1.0 - lb) * jax.nn.sigmoid(-fp)
    v = uh_ref[0, :, H_WIDTH:2 * H_WIDTH].astype(F32)

    G = R // H_SUB
    HALF = H_SUB // 2

    def halves(x):
        x4 = x.reshape(G, 2, HALF, H_WIDTH)
        return x4[:, 0], x4[:, 1]

    def whole(lo, hi):
        return jnp.stack([lo, hi], axis=1).reshape(R, H_WIDTH)

    sub = lax.broadcasted_iota(jnp.int32, (G, HALF, H_WIDTH), 1)

    def cumsum_half(x):
        shift = 1
        while shift < HALF:
            x = x + jnp.where(sub >= shift, pltpu.roll(x, shift, 1), 0.0)
            shift *= 2
        return x

    lf_lo, lf_hi = halves(logf)
    bc_lo = cumsum_half(lf_lo)
    bc_hi = cumsum_half(lf_hi) + bc_lo[:, HALF - 1:HALF, :]
    bc = whole(bc_lo, bc_hi)
    q_lo, q_hi = halves(q)
    k_lo, k_hi = halves(k)
    v_lo, v_hi = halves(v)

    ones_bd = _block_ones(LANES, HEAD_DIM)

    def weights(qq, bq, bs, ks, valid):
        dlt = bq - bs
        if valid is not None:
            dlt = jnp.where(valid, dlt, NEG_BIG)
        return (qq * ks * jnp.exp(dlt)).reshape(G * HALF, H_WIDTH)

    o_lo = jnp.zeros((G, HALF, H_WIDTH), F32)
    o_hi = jnp.zeros((G, HALF, H_WIDTH), F32)
    for b in range(HALF):
        if b == 0:
            valid = None
            bs_l, ks_l, vs_l = bc_lo, k_lo, v_lo
            bs_h, ks_h, vs_h = bc_hi, k_hi, v_hi
        else:
            valid = sub >= b
            bs_l, ks_l, vs_l = (pltpu.roll(x, b, 1) for x in (bc_lo, k_lo, v_lo))
            bs_h, ks_h, vs_h = (jnp.where(valid, pltpu.roll(x, b, 1), y)
                                for x, y in ((bc_hi, bs_l), (k_hi, ks_l), (v_hi, vs_l)))
        pr = jnp.concatenate([weights(q_lo, bc_lo, bs_l, ks_l, valid),
                              weights(q_hi, bc_hi, bs_h, ks_h, None),
                              weights(q_hi, bc_hi, bs_l, ks_l, valid)], axis=0).astype(BF16)
        red = jnp.concatenate([_dot(pr[:, 0:LANES], ones_bd), _dot(pr[:, LANES:], ones_bd)], axis=1)
        red = red.reshape(3, G, HALF, H_WIDTH)
        o_lo = o_lo + red[0] * vs_l
        o_hi = o_hi + red[1] * vs_h + red[2] * vs_l
    o = whole(o_lo, o_hi)

    bl_rows = jnp.concatenate(
        [jnp.broadcast_to(bc[(j + 1) * H_SUB - 1:(j + 1) * H_SUB, :], (H_SUB, H_WIDTH)) for j in range(NS)], axis=0)
    qe = q * jnp.exp(bc)
    ke = k * jnp.exp(bl_rows - bc)
    tcol = lax.broadcasted_iota(jnp.int32, (LANES, R), 1) // H_SUB
    trow = lax.broadcasted_iota(jnp.int32, (R, LANES), 0) // H_SUB
    sr = lax.broadcasted_iota(jnp.int32, (LANES, LANES), 0) < HEAD_DIM
    s_block = sr == (lax.broadcasted_iota(jnp.int32, (LANES, LANES), 1) < HEAD_DIM)
    outs = []
    for p in range(2):
        sl = slice(p * LANES, (p + 1) * LANES)
        v_t = v[:, sl].T
        v_stack = jnp.concatenate([jnp.where(tcol == j, v_t, 0.0) for j in range(NS)], axis=0).astype(BF16)
        upd = _dot(v_stack, ke[:, sl].astype(BF16))
        st = st_ref[p]
        for j in range(NS):
            w_ref[:, j * LANES:(j + 1) * LANES] = st
            a_row = jnp.exp(bc[(j + 1) * H_SUB - 1:(j + 1) * H_SUB, sl])
            st = st * a_row + jnp.where(s_block, upd[j * LANES:(j + 1) * LANES, :], 0.0)
        st_ref[p] = st
        q_exp = jnp.concatenate([jnp.where(trow == j, qe[:, sl], 0.0) for j in range(NS)], axis=1).astype(BF16)
        outs.append(_nt_dot(q_exp, w_ref[...].astype(BF16)))
    o = o + jnp.concatenate(outs, axis=1)

    ms = jnp.concatenate([_group_mean(o[:, 0:LANES] * o[:, 0:LANES], ones_bd, HEAD_DIM),
                          _group_mean(o[:, LANES:] * o[:, LANES:], ones_bd, HEAD_DIM)], axis=1)
    hg = uh_ref[0, :, 2 * H_WIDTH:3 * H_WIDTH].astype(F32)
    y = (o * lax.rsqrt(ms + EPS) * ng_ref[...]) * (hg * jax.nn.sigmoid(hg))
    y_ref[0] = y.astype(y_ref.dtype)


def _hgrn(uh, uf, lb, ng):
    b, t, _ = uh.shape
    ns = H_BLOCK // H_SUB
    return pl.pallas_call(
        _hgrn_kernel,
        grid=(b, t // H_BLOCK),
        in_specs=[pl.BlockSpec((1, H_BLOCK, 3 * H_WIDTH), lambda i, c: (i, c, 0)),
                  pl.BlockSpec((1, H_BLOCK, H_WIDTH), lambda i, c: (i, c, 0)),
                  pl.BlockSpec((1, H_WIDTH), lambda i, c: (0, 0)),
                  pl.BlockSpec((1, H_WIDTH), lambda i, c: (0, 0))],
        out_specs=pl.BlockSpec((1, H_BLOCK, H_WIDTH), lambda i, c: (i, c, 0)),
        out_shape=jax.ShapeDtypeStruct((b, t, H_WIDTH), BF16),
        scratch_shapes=[pltpu.VMEM((LANES, ns * LANES), F32),
                        pltpu.VMEM((2, LANES, LANES), F32)],
        compiler_params=pltpu.CompilerParams(dimension_semantics=("arbitrary", "arbitrary"),
                                             vmem_limit_bytes=VMEM_LIMIT),
        name="hgrn2",
    )(uh, uf, lb, ng)


def _attn_kernel(tab_ref, lam_ref, q_ref, k_ref, v_ref, sg_ref, y_ref,
                 qt_ref, vt_ref, d_ref, s0_ref, s1_ref, p0_ref, p1_ref, a0_ref, a1_ref, m_ref, acc_ref, *,
                 out_scale, n_steps):
    n_q = q_ref.shape[1] // Q_TILE
    lane = lax.broadcasted_iota(jnp.int32, (Q_TILE, LANES), 1)
    for i in range(n_q):
        q = q_ref[0, i * Q_TILE:(i + 1) * Q_TILE, :].astype(F32)
        qt_ref[i, :, 0:Q_TILE] = jnp.where(lane < HEAD_DIM, q, 0.0).T.astype(BF16)
        qt_ref[i, :, Q_TILE:] = jnp.where(lane >= HEAD_DIM, q, 0.0).T.astype(BF16)
    for j in range(vt_ref.shape[0]):
        vt_ref[j, 0:LANES, :] = v_ref[0, j * K_TILE:(j + 1) * K_TILE, :].astype(F32).T.astype(BF16)
        vt_ref[j, LANES:, :] = jnp.ones((LANES, K_TILE), BF16)
    c = lax.broadcasted_iota(jnp.int32, (K_TILE, 2 * Q_TILE), 1)
    d_ref[...] = jnp.where(c >= Q_TILE, c - Q_TILE, c) - lax.broadcasted_iota(jnp.int32, (K_TILE, 2 * Q_TILE), 0)

    def scores(t, s_ref):
        kstart = pl.multiple_of(tab_ref[1, t] * K_TILE, K_TILE)
        kb = k_ref[0, pl.ds(kstart, K_TILE), :]
        for hcol in range(2):
            cs = slice(hcol * Q_TILE, (hcol + 1) * Q_TILE)
            s_ref[:, cs] = _dot(kb, qt_ref[tab_ref[0, t], :, cs])

    def numerators(t, s_ref, p_ref, a_ref):
        visible = d_ref[...] >= tab_ref[1, t] * K_TILE - tab_ref[0, t] * Q_TILE
        s = jnp.where(visible, s_ref[...], NEG_BIG)
        m_old = m_ref[...]
        m_new = jnp.maximum(m_old, jnp.max(s, axis=0, keepdims=True))
        p_ref[...] = jnp.exp2(s - m_new).astype(BF16)
        a_ref[...] = jnp.exp2(m_old - m_new)
        m_ref[...] = jnp.where(tab_ref[2, t] == 1, NEG_BIG, m_new)

    def values(t, p_ref, a_ref):
        vt = vt_ref[tab_ref[1, t]]
        pv = jnp.concatenate([_dot(vt, p_ref[:, 0:Q_TILE]), _dot(vt, p_ref[:, Q_TILE:])], axis=1)
        acc = a_ref[...] * acc_ref[...] + pv
        acc_ref[...] = acc
        on = acc[0:LANES, :] / acc[LANES:LANES + 1, :]
        o_t = on[:, :Q_TILE] - lam_ref[0] * on[:, Q_TILE:]
        ms = jnp.mean(o_t * o_t, axis=0, keepdims=True)
        y = ((o_t * lax.rsqrt(ms + EPS)).T * sg_ref[...]) * out_scale
        y_ref[0, pl.ds(pl.multiple_of(tab_ref[0, t] * Q_TILE, Q_TILE), Q_TILE), :] = y.astype(y_ref.dtype)

    m_ref[...] = jnp.full(m_ref.shape, NEG_BIG, F32)
    acc_ref[...] = jnp.zeros(acc_ref.shape, F32)
    scores(0, s0_ref)
    scores(1, s1_ref)
    numerators(0, s0_ref, p0_ref, a0_ref)

    bufs = ((s0_ref, p0_ref, a0_ref), (s1_ref, p1_ref, a1_ref))
    unroll = 4 if n_steps % 4 == 0 else 2

    def body(i, carry):
        for u in range(unroll):
            t = unroll * i + u
            cur, nxt = bufs[u % 2], bufs[(u + 1) % 2]
            scores(t + 2, cur[0])
            numerators(t + 1, *nxt)
            values(t, cur[1], cur[2])
        return carry

    lax.fori_loop(0, n_steps // unroll, body, 0)


def _attn_schedule(t):
    cols = []
    for qi in range(t // Q_TILE):
        n_kv = (qi * Q_TILE) // K_TILE + 1
        cols += [(qi, kj, int(kj == n_kv - 1)) for kj in range(n_kv)]
    assert len(cols) % 2 == 0
    n_steps = len(cols)
    cols += [(0, 0, 0)] * 2
    return n_steps, jnp.asarray(cols, jnp.int32).T


def _attn(lam, qh, kh, vh, sg, out_scale):
    b, t, _ = qh.shape
    n_steps, tab = _attn_schedule(t)
    seq = pl.BlockSpec((1, t, LANES), lambda i, h: (i, 0, h))
    smem = pl.BlockSpec(memory_space=pltpu.SMEM)
    cols = 2 * Q_TILE
    return pl.pallas_call(
        functools.partial(_attn_kernel, out_scale=out_scale, n_steps=n_steps),
        grid=(b, A_HEADS),
        in_specs=[smem, smem, seq, seq, seq, pl.BlockSpec((1, LANES), lambda i, h: (0, 0))],
        out_specs=seq,
        out_shape=jax.ShapeDtypeStruct((b, t, A_WIDTH), BF16),
        scratch_shapes=[pltpu.VMEM((t // Q_TILE, LANES, cols), BF16),
                        pltpu.VMEM((t // K_TILE, 2 * LANES, K_TILE), BF16),
                        pltpu.VMEM((K_TILE, cols), jnp.int32),
                        pltpu.VMEM((K_TILE, cols), F32), pltpu.VMEM((K_TILE, cols), F32),
                        pltpu.VMEM((K_TILE, cols), BF16), pltpu.VMEM((K_TILE, cols), BF16),
                        pltpu.VMEM((1, cols), F32), pltpu.VMEM((1, cols), F32),
                        pltpu.VMEM((1, cols), F32),
                        pltpu.VMEM((2 * LANES, cols), F32)],
        compiler_params=pltpu.CompilerParams(dimension_semantics=("arbitrary", "arbitrary"),
                                             vmem_limit_bytes=VMEM_LIMIT),
        name="diff_attn",
    )(tab, lam, qh, kh, vh, sg)


def _rope_tables(t):
    half = ROT_DIM // 2
    inv = ROPE_THETA ** (-jnp.arange(half, dtype=F32) / half)
    ang = jnp.arange(t, dtype=F32)[:, None] * inv[None, :]
    cos, sin = jnp.cos(ang), jnp.sin(ang)
    pad = jnp.zeros((t, HEAD_DIM - ROT_DIM), F32)
    zero = jnp.zeros((t, half), F32)
    c = jnp.concatenate([cos, cos, pad + 1.0], axis=1)
    s1 = jnp.concatenate([-sin, zero, pad], axis=1)
    s2 = jnp.concatenate([zero, sin, pad], axis=1)
    tile = lambda a: jnp.concatenate([a, a], axis=1)
    return tile(c), tile(s1), tile(s2)


def kernel(x, norm1_g, w_in, mlstm_conv_w, mlstm_gate_b, mlstm_norm_g, diff_q_norm_g, diff_k_norm_g,
           diff_lambda, diff_subln_g, hgrn_lb_param, hgrn_norm_g, w_out, norm2_g, w_up, w_down):
    b, t, d = x.shape
    n = b * t
    assert d == D_MODEL and t % max(ROW_TILE, M_BLOCK, H_BLOCK, Q_TILE, K_TILE) == 0
    assert K_TILE % Q_TILE == 0

    sm = jax.nn.softmax(hgrn_lb_param.astype(F32), axis=0)
    lower_bounds = jnp.cumsum(sm, axis=0) - sm[0]
    cos, s1, s2 = _rope_tables(t)
    c_g = 4 * M_WIDTH
    c_a = c_g + 8
    c_h = c_a + 3 * A_WIDTH
    head_tile = lambda g, width: jnp.tile(g.astype(F32), width // HEAD_DIM).reshape(1, width)

    x2 = x.reshape(n, d)
    for l in range(DEPTH):
        lambda_init = 0.8 - 0.6 * math.exp(-0.3 * l)
        w = w_in[l]
        cols = lambda lo, hi: w[:, lo:hi].astype(BF16)
        wg = jnp.pad(w[:, c_g:c_a], ((0, 0), (0, LANES - 8))).astype(BF16)
        wgt = jnp.pad(w[:, c_g:c_a].T, ((0, 8), (0, 0))).astype(BF16)
        gate_b = mlstm_gate_b[l].astype(F32).reshape(1, 8)
        bg = jnp.pad(gate_b, ((0, 0), (0, LANES - 8)))
        bgt = jnp.broadcast_to(jnp.pad(gate_b.reshape(8, 1), ((0, 8), (0, 0))), (16, LANES))
        wh = jnp.concatenate([w[:, c_h:c_h + H_WIDTH], w[:, c_h + 2 * H_WIDTH:]], axis=1).astype(BF16)
        um, qh, kh, vh, uh, uf, gc, gt = _inproj(
            x2, norm1_g[l].reshape(1, d), cols(0, c_g),
            cols(c_a, c_a + A_WIDTH), cols(c_a + A_WIDTH, c_a + 2 * A_WIDTH), cols(c_a + 2 * A_WIDTH, c_h),
            wh, cols(c_h + H_WIDTH, c_h + 2 * H_WIDTH), wg, wgt, bg, bgt,
            head_tile(diff_q_norm_g[l], A_WIDTH), head_tile(diff_k_norm_g[l], A_WIDTH), cos, s1, s2)

        ym = _mlstm(um.reshape(b, t, -1), gc.reshape(b, t, LANES), gt, mlstm_conv_w[l].astype(F32),
                    head_tile(mlstm_norm_g[l], LANES))

        lv = diff_lambda[l].astype(F32)
        lam = jnp.exp(jnp.sum(lv[0] * lv[1])) - jnp.exp(jnp.sum(lv[2] * lv[3])) + lambda_init
        ya = _attn(lam.reshape(1), qh.reshape(b, t, -1), kh.reshape(b, t, -1), vh.reshape(b, t, -1),
                   diff_subln_g[l].astype(F32).reshape(1, LANES), 1.0 - lambda_init)

        yh = _hgrn(uh.reshape(b, t, -1), uf.reshape(b, t, -1), lower_bounds[l].reshape(1, H_WIDTH),
                   head_tile(hgrn_norm_g[l], H_WIDTH))

        x2 = _post(x2, ym.reshape(n, -1), ya.reshape(n, -1), yh.reshape(n, -1), w_out[l].astype(BF16),
                   norm2_g[l].reshape(1, d), w_up[l].astype(BF16), w_down[l].astype(BF16))
    return x2.reshape(b, t, d)
```

```python
import functools
import math

import jax
import jax.numpy as jnp
from jax import lax
from jax.experimental import pallas as pl
from jax.experimental.pallas import tpu as pltpu

F32 = jnp.float32
BF16 = jnp.bfloat16

D_MODEL = 1024
DEPTH = 2
D_FF = 4 * D_MODEL
EPS = 1e-6
NEG_BIG = -1e30
HEAD_DIM = 64
M_WIDTH = 256
M_CONV = 4
A_WIDTH = 512
A_HEADS = 4
ROT_DIM = 16
ROPE_THETA = 500000.0
H_WIDTH = 256
LANES = 128

VMEM_LIMIT = 56 * 1024 * 1024

ROW_TILE = 512
COL_CHUNK = 512
M_CHUNK = 128
R_BLOCK = 256
H_SUB = 16
Q_TILE = 256
K_TILE = 512


def _nt_dot(a, b):
    return lax.dot_general(a, b, (((1,), (1,)), ((), ())), preferred_element_type=F32)


def _dot(a, b):
    return jnp.dot(a, b, preferred_element_type=F32)


def _block_ones(n, group):
    r = lax.broadcasted_iota(jnp.int32, (n, n), 0) // group
    c = lax.broadcasted_iota(jnp.int32, (n, n), 1) // group
    return jnp.where(r == c, 1.0, 0.0).astype(BF16)


def _group_mean(xsq, ones_bd, group):
    return _dot(xsq.astype(BF16), ones_bd) * (1.0 / group)


def _log_sigmoid(x):
    return jnp.minimum(x, 0.0) - jnp.log1p(jnp.exp(-jnp.abs(x)))


def _inproj_kernel(x_ref, g_ref, wm_ref, wq_ref, wk_ref, wv_ref, wh_ref, wf_ref, wg_ref, wgt_ref, bg_ref, bgt_ref,
                   qg_ref, kg_ref, cos_ref, s1_ref, s2_ref,
                   um_ref, q_ref, k_ref, v_ref, uh_ref, uf_ref, gc_ref, gt_ref):
    x = x_ref[...]
    ms = jnp.mean(x * x, axis=-1, keepdims=True)
    h = (x * lax.rsqrt(ms + EPS) * g_ref[...]).astype(BF16)

    def project(w_ref, o_ref):
        width = w_ref.shape[1]
        for c0 in range(0, width, COL_CHUNK):
            sl = slice(c0, min(c0 + COL_CHUNK, width))
            o_ref[:, sl] = _dot(h, w_ref[:, sl]).astype(o_ref.dtype)

    project(wm_ref, um_ref)
    project(wv_ref, v_ref)
    project(wh_ref, uh_ref)
    project(wf_ref, uf_ref)
    gc_ref[...] = _dot(h, wg_ref[...]) + bg_ref[...]
    gt_ref[...] = _nt_dot(wgt_ref[...], h) + bgt_ref[:, 0:1]

    ones_bd = _block_ones(LANES, HEAD_DIM)
    reps = A_WIDTH // LANES
    cos = jnp.concatenate([cos_ref[...]] * reps, axis=1)
    s1 = jnp.concatenate([s1_ref[...]] * reps, axis=1)
    s2 = jnp.concatenate([s2_ref[...]] * reps, axis=1)

    def norm_rope(u, gain):
        ms_g = jnp.concatenate(
            [_group_mean(u[:, i * LANES:(i + 1) * LANES] * u[:, i * LANES:(i + 1) * LANES], ones_bd, HEAD_DIM)
             for i in range(reps)], axis=1)
        un = u * lax.rsqrt(ms_g + EPS) * gain
        return un * cos + pltpu.roll(un, A_WIDTH - ROT_DIM // 2, 1) * s1 + pltpu.roll(un, ROT_DIM // 2, 1) * s2

    q = norm_rope(_dot(h, wq_ref[...]), qg_ref[...]) * (HEAD_DIM ** -0.5 * math.log2(math.e))
    q_ref[...] = q.astype(BF16)
    k_ref[...] = norm_rope(_dot(h, wk_ref[...]), kg_ref[...]).astype(BF16)


def _inproj(x2, g, wm, wq, wk, wv, wh, wf, wg, wgt, bg, bgt, qg, kg, cos, s1, s2):
    n = x2.shape[0]
    t_blocks = cos.shape[0] // ROW_TILE
    const = lambda a: pl.BlockSpec(a.shape, lambda i: (0, 0))
    rows = lambda width: pl.BlockSpec((ROW_TILE, width), lambda i: (i, 0))
    tab = pl.BlockSpec((ROW_TILE, LANES), lambda i: (i % t_blocks, 0))
    out = lambda width, dtype: jax.ShapeDtypeStruct((n, width), dtype)
    return pl.pallas_call(
        _inproj_kernel,
        grid=(n // ROW_TILE,),
        in_specs=[rows(D_MODEL)] + [const(a) for a in (g, wm, wq, wk, wv, wh, wf, wg, wgt, bg, bgt, qg, kg)]
                 + [tab, tab, tab],
        out_specs=[rows(wm.shape[1]), rows(A_WIDTH), rows(A_WIDTH), rows(A_WIDTH), rows(wh.shape[1]),
                   rows(wf.shape[1]), rows(LANES), pl.BlockSpec((16, ROW_TILE), lambda i: (0, i))],
        out_shape=[out(wm.shape[1], BF16), out(A_WIDTH, BF16), out(A_WIDTH, BF16), out(A_WIDTH, BF16),
                   out(wh.shape[1], BF16), out(wf.shape[1], F32), out(LANES, F32),
                   jax.ShapeDtypeStruct((16, n), F32)],
        compiler_params=pltpu.CompilerParams(dimension_semantics=("arbitrary",), vmem_limit_bytes=VMEM_LIMIT),
        name="inproj",
    )(x2, g, wm, wq, wk, wv, wh, wf, wg, wgt, bg, bgt, qg, kg, cos, s1, s2)


def _post_kernel(x_ref, ym_ref, ya_ref, yh_ref, wo_ref, g2_ref, wu_ref, wd_ref, o_ref):
    x1 = x_ref[...]
    x1 = x1 + _dot(ym_ref[...], wo_ref[0:M_WIDTH, :])
    x1 = x1 + _dot(ya_ref[...], wo_ref[M_WIDTH:M_WIDTH + A_WIDTH, :])
    x1 = x1 + _dot(yh_ref[...], wo_ref[M_WIDTH + A_WIDTH:, :])
    ms = jnp.mean(x1 * x1, axis=-1, keepdims=True)
    h2 = (x1 * lax.rsqrt(ms + EPS) * g2_ref[...]).astype(BF16)
    o_ref[...] = x1
    for c in range(D_FF // COL_CHUNK):
        sl = slice(c * COL_CHUNK, (c + 1) * COL_CHUNK)
        a = jnp.maximum(_dot(h2, wu_ref[:, sl]), 0.0)
        o_ref[...] += _dot((a * a).astype(BF16), wd_ref[sl, :])


def _post(x2, ym, ya, yh, wo, g2, wu, wd):
    n = x2.shape[0]
    const = lambda shape: pl.BlockSpec(shape, lambda i: (0, 0), pipeline_mode=pl.Buffered(1))
    rows = lambda width: pl.BlockSpec((ROW_TILE, width), lambda i: (i, 0))
    return pl.pallas_call(
        _post_kernel,
        grid=(n // ROW_TILE,),
        in_specs=[rows(D_MODEL), rows(M_WIDTH), rows(A_WIDTH), rows(H_WIDTH),
                  const(wo.shape), const((1, D_MODEL)), const(wu.shape), const(wd.shape)],
        out_specs=rows(D_MODEL),
        out_shape=jax.ShapeDtypeStruct((n, D_MODEL), F32),
        compiler_params=pltpu.CompilerParams(dimension_semantics=("arbitrary",), vmem_limit_bytes=VMEM_LIMIT),
        name="post",
    )(x2, ym, ya, yh, wo, g2, wu, wd)


def _mlstm_stream(um_ref, gc_ref, gt_ref, cw_ref, ng_ref, y_ref, tail_ref, gn_ref, m_ref):
    L = M_CHUNK
    R = R_BLOCK

    xin = um_ref[0, :, 0:2 * M_WIDTH].astype(F32)
    x3 = xin.reshape(R // 8, 8, 2 * M_WIDTH)
    tail = tail_ref[...].reshape(1, 8, 2 * M_WIDTH)
    sub = lax.broadcasted_iota(jnp.int32, x3.shape, 1)
    conv = None
    for j in range(M_CONV - 1):
        back = M_CONV - 1 - j
        rolled = pltpu.roll(x3, back, 1)
        prev = jnp.concatenate([pltpu.roll(tail, back, 1), rolled[:-1]], axis=0)
        term = jnp.where(sub >= back, rolled, prev) * cw_ref[j:j + 1, :]
        conv = term if conv is None else conv + term
    conv = (conv + x3 * cw_ref[M_CONV - 1:M_CONV, :]).reshape(R, 2 * M_WIDTH)
    tail_ref[...] = xin[R - 8:R, :]
    qk = conv * jax.nn.sigmoid(conv)

    gcol_all = gc_ref[0]
    grow_all = gt_ref[...]
    logf_col_all = _log_sigmoid(gcol_all)
    logf_row_all = _log_sigmoid(grow_all)

    ti = lax.broadcasted_iota(jnp.int32, (L, L), 0)
    si = lax.broadcasted_iota(jnp.int32, (L, L), 1)
    causal = si <= ti
    lane = lax.broadcasted_iota(jnp.int32, (L, LANES), 1)
    left = lane < HEAD_DIM
    ones_bd = _block_ones(LANES, HEAD_DIM)
    gr = lax.broadcasted_iota(jnp.int32, (LANES, 2 * LANES), 0)
    gc_i = lax.broadcasted_iota(jnp.int32, (LANES, 2 * LANES), 1)
    g_rowleft = gr < HEAD_DIM
    g_block = g_rowleft == ((gc_i % LANES) < HEAD_DIM)
    yield

    for ci in range(R // L):
        rows = slice(ci * L, (ci + 1) * L)
        gcol, grow = gcol_all[rows], grow_all[:, rows]
        logf_col, logf_row = logf_col_all[rows], logf_row_all[:, rows]
        for p in range(2):
            sl = slice(p * LANES, (p + 1) * LANES)
            q_pair = qk[rows, p * LANES:(p + 1) * LANES]
            k_pair = qk[rows, M_WIDTH + p * LANES:M_WIDTH + (p + 1) * LANES] * (HEAD_DIM ** -0.5)
            v_pair = um_ref[0, rows, 2 * M_WIDTH + p * LANES:2 * M_WIDTH + (p + 1) * LANES]
            o_pair = um_ref[0, rows, 3 * M_WIDTH + p * LANES:3 * M_WIDTH + (p + 1) * LANES].astype(F32)
            v1 = jnp.concatenate([v_pair, jnp.ones((L, LANES), v_pair.dtype)], axis=1)
            k_bf = k_pair.astype(BF16)
            intra, scale, einv, wcol, decay = [], [], [], [], []
            for hh in range(2):
                h = 2 * p + hh
                m_prev = m_ref[h:h + 1, 0:1]
                i_row = grow[h:h + 1, :]
                i_col = gcol[:, h:h + 1]
                lf_row = logf_row[4 + h:5 + h, :]
                lf_col = logf_col[:, 4 + h:5 + h]
                b_col = jnp.sum(jnp.where(causal, lf_row, 0.0), axis=1, keepdims=True)
                b_row = jnp.sum(jnp.where(ti <= si, lf_col, 0.0), axis=0, keepdims=True)
                a_row = i_row - b_row
                a_col = i_col - b_col
                dm = jnp.where(causal, a_row, NEG_BIG)
                m_col = jnp.maximum(jnp.max(dm, axis=1, keepdims=True), m_prev)
                e = jnp.exp(dm - m_col)
                qm = jnp.where(left if hh == 0 else jnp.logical_not(left), q_pair, 0.0).astype(BF16)
                s = _nt_dot(qm, k_bf) * e
                intra.append(_dot(s.astype(BF16), v1))
                m_last = m_col[L - 1:L, :]
                scale.append(jnp.exp(m_prev - m_col))
                einv.append(jnp.exp(-(b_col + m_col)))
                wcol.append(jnp.exp(a_col - m_last))
                decay.append(jnp.exp(m_prev - m_last))
                m_ref[h:h + 1, :] = jnp.broadcast_to(b_col[L - 1:L, :] + m_last, (1, LANES))

            gn = gn_ref[p]
            inter = _dot(q_pair.astype(BF16), gn.astype(BF16))
            sc = jnp.where(left, scale[0], scale[1])
            num = jnp.where(left, intra[0][:, :LANES], intra[1][:, :LANES]) + sc * inter[:, :LANES]
            den = jnp.where(left, intra[0][:, LANES:], intra[1][:, LANES:]) + sc * inter[:, LANES:]
            hv = num / jnp.maximum(jnp.abs(den), jnp.where(left, einv[0], einv[1]))

            kw = k_pair * jnp.where(left, wcol[0], wcol[1])
            upd = _dot(kw.T.astype(BF16), v1)
            gn_ref[p] = jnp.where(g_rowleft, decay[0], decay[1]) * gn + jnp.where(g_block, upd, 0.0)

            ms = _group_mean(hv * hv, ones_bd, HEAD_DIM)
            y = jax.nn.sigmoid(o_pair) * (hv * lax.rsqrt(ms + EPS) * ng_ref[...])
            y_ref[0, rows, sl] = y.astype(y_ref.dtype)
            yield


def _hgrn_stream(uh_ref, uf_ref, lb_ref, ng_ref, y_ref, w_ref, st_ref):
    R = R_BLOCK
    NS = R // H_SUB

    lb = lb_ref[...]
    fp = uf_ref[0]
    hq = uh_ref[0, :, 0:H_WIDTH].astype(F32)
    q = hq * jax.nn.sigmoid(hq)
    logf = jnp.log(lb + (1.0 - lb) * jax.nn.sigmoid(fp)) * math.log2(math.e)
    k = (1.0 - lb) * jax.nn.sigmoid(-fp)
    v = uh_ref[0, :, H_WIDTH:2 * H_WIDTH].astype(F32)

    G = R // H_SUB
    HALF = H_SUB // 2

    def halves(x):
        x4 = x.reshape(G, 2, HALF, H_WIDTH)
        return x4[:, 0], x4[:, 1]

    def whole(lo, hi):
        return jnp.stack([lo, hi], axis=1).reshape(R, H_WIDTH)

    sub = lax.broadcasted_iota(jnp.int32, (G, HALF, H_WIDTH), 1)

    def cumsum_half(x):
        shift = 1
        while shift < HALF:
            x = x + jnp.where(sub >= shift, pltpu.roll(x, shift, 1), 0.0)
            shift *= 2
        return x

    lf_lo, lf_hi = halves(logf)
    bc_lo = cumsum_half(lf_lo)
    bc_hi = cumsum_half(lf_hi) + bc_lo[:, HALF - 1:HALF, :]
    bc = whole(bc_lo, bc_hi)
    q_lo, q_hi = halves(q)
    k_lo, k_hi = halves(k)
    v_lo, v_hi = halves(v)
    ones_bd = _block_ones(LANES, HEAD_DIM)

    bl_rows = jnp.concatenate(
        [jnp.broadcast_to(bc[(j + 1) * H_SUB - 1:(j + 1) * H_SUB, :], (H_SUB, H_WIDTH)) for j in range(NS)], axis=0)
    qe = q * jnp.exp2(bc)
    ke = k * jnp.exp2(bl_rows - bc)
    tcol = lax.broadcasted_iota(jnp.int32, (LANES, R), 1) // H_SUB
    trow = lax.broadcasted_iota(jnp.int32, (R, LANES), 0) // H_SUB
    sr = lax.broadcasted_iota(jnp.int32, (LANES, LANES), 0) < HEAD_DIM
    s_block = sr == (lax.broadcasted_iota(jnp.int32, (LANES, LANES), 1) < HEAD_DIM)
    yield

    outs = []

    def history(p):
        sl = slice(p * LANES, (p + 1) * LANES)
        v_t = v[:, sl].T
        v_stack = jnp.concatenate([jnp.where(tcol == j, v_t, 0.0) for j in range(NS)], axis=0).astype(BF16)
        upd = _dot(v_stack, ke[:, sl].astype(BF16))
        yield
        st = st_ref[p]
        for j in range(NS):
            w_ref[p, :, j * LANES:(j + 1) * LANES] = st
            a_row = jnp.exp2(bc[(j + 1) * H_SUB - 1:(j + 1) * H_SUB, sl])
            st = st * a_row + jnp.where(s_block, upd[j * LANES:(j + 1) * LANES, :], 0.0)
        st_ref[p] = st
        q_exp = jnp.concatenate([jnp.where(trow == j, qe[:, sl], 0.0) for j in range(NS)], axis=1).astype(BF16)
        outs.append(_nt_dot(q_exp, w_ref[p].astype(BF16)))
        yield

    def weights(qq, bq, bs, ks, valid):
        dlt = bq - bs
        if valid is not None:
            dlt = jnp.where(valid, dlt, NEG_BIG)
        return (qq * ks * jnp.exp2(dlt)).reshape(G * HALF, H_WIDTH)

    acc = [jnp.zeros((G, HALF, H_WIDTH), F32), jnp.zeros((G, HALF, H_WIDTH), F32)]

    def direct():
        for b in range(HALF):
            if b == 0:
                valid = None
                bs_l, ks_l, vs_l = bc_lo, k_lo, v_lo
                bs_h, ks_h, vs_h = bc_hi, k_hi, v_hi
            else:
                valid = sub >= b
                bs_l, ks_l, vs_l = (pltpu.roll(x, b, 1) for x in (bc_lo, k_lo, v_lo))
                bs_h, ks_h, vs_h = (jnp.where(valid, pltpu.roll(x, b, 1), y)
                                    for x, y in ((bc_hi, bs_l), (k_hi, ks_l), (v_hi, vs_l)))
            pr = jnp.concatenate([weights(q_lo, bc_lo, bs_l, ks_l, valid),
                                  weights(q_hi, bc_hi, bs_h, ks_h, None),
                                  weights(q_hi, bc_hi, bs_l, ks_l, valid)], axis=0).astype(BF16)
            red = jnp.concatenate([_dot(pr[:, 0:LANES], ones_bd), _dot(pr[:, LANES:], ones_bd)], axis=1)
            red = red.reshape(3, G, HALF, H_WIDTH)
            acc[0] = acc[0] + red[0] * vs_l
            acc[1] = acc[1] + red[1] * vs_h + red[2] * vs_l
            yield

    hist = [history(0), history(1)]
    dire = direct()
    for piece in range(HALF):
        if piece % 2 == 0:
            next(hist[piece // 4])
            yield
        next(dire)
        yield
    o = whole(acc[0], acc[1]) + jnp.concatenate(outs, axis=1)

    ms = jnp.concatenate([_group_mean(o[:, 0:LANES] * o[:, 0:LANES], ones_bd, HEAD_DIM),
                          _group_mean(o[:, LANES:] * o[:, LANES:], ones_bd, HEAD_DIM)], axis=1)
    hg = uh_ref[0, :, 2 * H_WIDTH:3 * H_WIDTH].astype(F32)
    y = (o * lax.rsqrt(ms + EPS) * ng_ref[...]) * (hg * jax.nn.sigmoid(hg))
    y_ref[0] = y.astype(y_ref.dtype)
    yield


def _recurrent_kernel(um_ref, gc_ref, gt_ref, cw_ref, mg_ref, uh_ref, uf_ref, lb_ref, hg_ref, ym_ref, yh_ref,
                      tail_ref, gn_ref, m_ref, w_ref, st_ref):
    @pl.when(pl.program_id(1) == 0)
    def _():
        tail_ref[...] = jnp.zeros(tail_ref.shape, F32)
        gn_ref[...] = jnp.zeros(gn_ref.shape, F32)
        m_ref[...] = jnp.zeros(m_ref.shape, F32)
        st_ref[...] = jnp.zeros(st_ref.shape, F32)

    mlstm = _mlstm_stream(um_ref, gc_ref, gt_ref, cw_ref, mg_ref, ym_ref, tail_ref, gn_ref, m_ref)
    hgrn = _hgrn_stream(uh_ref, uf_ref, lb_ref, hg_ref, yh_ref, w_ref, st_ref)
    live = [hgrn, mlstm]
    turns = (3, 1)
    while live:
        for g, n in zip(list(live), turns):
            for _ in range(n):
                if g in live and next(g, StopIteration) is StopIteration:
                    live.remove(g)


def _recurrent(um, gc, gt, conv_w, mg, uh, uf, lb, hg):
    b, t, _ = um.shape
    nb = t // R_BLOCK
    ns = R_BLOCK // H_SUB
    blk = lambda width: pl.BlockSpec((1, R_BLOCK, width), lambda i, c: (i, c, 0))
    const = lambda a: pl.BlockSpec(a.shape, lambda i, c: (0, 0))
    return pl.pallas_call(
        _recurrent_kernel,
        grid=(b, nb),
        in_specs=[blk(4 * M_WIDTH), blk(LANES), pl.BlockSpec((16, R_BLOCK), lambda i, c: (0, i * nb + c)),
                  const(conv_w), const(mg), blk(3 * H_WIDTH), blk(H_WIDTH), const(lb), const(hg)],
        out_specs=[blk(M_WIDTH), blk(H_WIDTH)],
        out_shape=[jax.ShapeDtypeStruct((b, t, M_WIDTH), BF16), jax.ShapeDtypeStruct((b, t, H_WIDTH), BF16)],
        scratch_shapes=[pltpu.VMEM((8, 2 * M_WIDTH), F32),
                        pltpu.VMEM((2, LANES, 2 * LANES), F32),
                        pltpu.VMEM((8, LANES), F32),
                        pltpu.VMEM((2, LANES, ns * LANES), F32),
                        pltpu.VMEM((2, LANES, LANES), F32)],
        compiler_params=pltpu.CompilerParams(dimension_semantics=("arbitrary", "arbitrary"),
                                             vmem_limit_bytes=VMEM_LIMIT),
        name="recurrent",
    )(um, gc, gt, conv_w, mg, uh, uf, lb, hg)


def _attn_kernel(tab_ref, lam_ref, q_ref, k_ref, v_ref, sg_ref, y_ref,
                 qt_ref, vt_ref, d_ref, s0_ref, s1_ref, p0_ref, p1_ref, a0_ref, a1_ref, m_ref, acc_ref, *,
                 out_scale, n_steps):
    n_q = q_ref.shape[1] // Q_TILE
    lane = lax.broadcasted_iota(jnp.int32, (Q_TILE, LANES), 1)
    for i in range(n_q):
        q = q_ref[0, i * Q_TILE:(i + 1) * Q_TILE, :].astype(F32)
        qt_ref[i, :, 0:Q_TILE] = jnp.where(lane < HEAD_DIM, q, 0.0).T.astype(BF16)
        qt_ref[i, :, Q_TILE:] = jnp.where(lane >= HEAD_DIM, q, 0.0).T.astype(BF16)
    for j in range(vt_ref.shape[0]):
        vt_ref[j, 0:LANES, :] = v_ref[0, j * K_TILE:(j + 1) * K_TILE, :].astype(F32).T.astype(BF16)
        vt_ref[j, LANES:, :] = jnp.ones((LANES, K_TILE), BF16)
    d_ref[...] = (lax.broadcasted_iota(jnp.int32, (K_TILE, Q_TILE), 1)
                  - lax.broadcasted_iota(jnp.int32, (K_TILE, Q_TILE), 0))

    halves = (slice(0, Q_TILE), slice(Q_TILE, 2 * Q_TILE))

    def scores(t, s_ref, cs):
        kstart = pl.multiple_of(tab_ref[1, t] * K_TILE, K_TILE)
        s_ref[:, cs] = _dot(k_ref[0, pl.ds(kstart, K_TILE), :], qt_ref[tab_ref[0, t], :, cs])

    def numerators(t, s_ref, p_ref, a_ref, cs):
        visible = d_ref[...] >= tab_ref[1, t] * K_TILE - tab_ref[0, t] * Q_TILE
        s = jnp.where(visible, s_ref[:, cs], NEG_BIG)
        m_old = m_ref[:, cs]
        m_new = jnp.maximum(m_old, jnp.max(s, axis=0, keepdims=True))
        p_ref[:, cs] = jnp.exp2(s - m_new).astype(BF16)
        a_ref[:, cs] = jnp.exp2(m_old - m_new)
        m_ref[:, cs] = jnp.where(tab_ref[2, t] == 1, NEG_BIG, m_new)

    def values(t, p_ref, a_ref, cs):
        acc = a_ref[:, cs] * acc_ref[:, cs] + _dot(vt_ref[tab_ref[1, t]], p_ref[:, cs])
        acc_ref[:, cs] = acc
        return acc[0:LANES, :] / acc[LANES:LANES + 1, :]

    def finish(t, on0, on1):
        o_t = on0 - lam_ref[0] * on1
        ms = jnp.mean(o_t * o_t, axis=0, keepdims=True)
        y = ((o_t * lax.rsqrt(ms + EPS)).T * sg_ref[...]) * out_scale
        y_ref[0, pl.ds(pl.multiple_of(tab_ref[0, t] * Q_TILE, Q_TILE), Q_TILE), :] = y.astype(y_ref.dtype)

    m_ref[...] = jnp.full(m_ref.shape, NEG_BIG, F32)
    acc_ref[...] = jnp.zeros(acc_ref.shape, F32)
    for cs in halves:
        scores(0, s0_ref, cs)
        scores(1, s1_ref, cs)
        numerators(0, s0_ref, p0_ref, a0_ref, cs)

    bufs = ((s0_ref, p0_ref, a0_ref), (s1_ref, p1_ref, a1_ref))
    unroll = 4 if n_steps % 4 == 0 else 2

    def body(i, carry):
        for u in range(unroll):
            t = unroll * i + u
            cur, nxt = bufs[u % 2], bufs[(u + 1) % 2]
            on = []
            for cs in halves:
                scores(t + 2, cur[0], cs)
                numerators(t + 1, *nxt, cs)
                on.append(values(t, cur[1], cur[2], cs))
            finish(t, *on)
        return carry

    lax.fori_loop(0, n_steps // unroll, body, 0)


def _attn_schedule(t):
    cols = []
    for qi in range(t // Q_TILE):
        n_kv = (qi * Q_TILE) // K_TILE + 1
        cols += [(qi, kj, int(kj == n_kv - 1)) for kj in range(n_kv)]
    assert len(cols) % 2 == 0
    n_steps = len(cols)
    cols += [(0, 0, 0)] * 2
    return n_steps, jnp.asarray(cols, jnp.int32).T


def _attn(lam, qh, kh, vh, sg, out_scale):
    b, t, _ = qh.shape
    n_steps, tab = _attn_schedule(t)
    seq = pl.BlockSpec((1, t, LANES), lambda i, h: (i, 0, h))
    smem = pl.BlockSpec(memory_space=pltpu.SMEM)
    cols = 2 * Q_TILE
    return pl.pallas_call(
        functools.partial(_attn_kernel, out_scale=out_scale, n_steps=n_steps),
        grid=(b, A_HEADS),
        in_specs=[smem, smem, seq, seq, seq, pl.BlockSpec((1, LANES), lambda i, h: (0, 0))],
        out_specs=seq,
        out_shape=jax.ShapeDtypeStruct((b, t, A_WIDTH), BF16),
        scratch_shapes=[pltpu.VMEM((t // Q_TILE, LANES, cols), BF16),
                        pltpu.VMEM((t // K_TILE, 2 * LANES, K_TILE), BF16),
                        pltpu.VMEM((K_TILE, Q_TILE), jnp.int32),
                        pltpu.VMEM((K_TILE, cols), F32), pltpu.VMEM((K_TILE, cols), F32),
                        pltpu.VMEM((K_TILE, cols), BF16), pltpu.VMEM((K_TILE, cols), BF16),
                        pltpu.VMEM((1, cols), F32), pltpu.VMEM((1, cols), F32),
                        pltpu.VMEM((1, cols), F32),
                        pltpu.VMEM((2 * LANES, cols), F32)],
        compiler_params=pltpu.CompilerParams(dimension_semantics=("arbitrary", "arbitrary"),
                                             vmem_limit_bytes=VMEM_LIMIT),
        name="diff_attn",
    )(tab, lam, qh, kh, vh, sg)


def _rope_tables(t):
    half = ROT_DIM // 2
    inv = ROPE_THETA ** (-jnp.arange(half, dtype=F32) / half)
    ang = jnp.arange(t, dtype=F32)[:, None] * inv[None, :]
    cos, sin = jnp.cos(ang), jnp.sin(ang)
    pad = jnp.zeros((t, HEAD_DIM - ROT_DIM), F32)
    zero = jnp.zeros((t, half), F32)
    c = jnp.concatenate([cos, cos, pad + 1.0], axis=1)
    s1 = jnp.concatenate([-sin, zero, pad], axis=1)
    s2 = jnp.concatenate([zero, sin, pad], axis=1)
    tile = lambda a: jnp.concatenate([a, a], axis=1)
    return tile(c), tile(s1), tile(s2)


def kernel(x, norm1_g, w_in, mlstm_conv_w, mlstm_gate_b, mlstm_norm_g, diff_q_norm_g, diff_k_norm_g,
           diff_lambda, diff_subln_g, hgrn_lb_param, hgrn_norm_g, w_out, norm2_g, w_up, w_down):
    b, t, d = x.shape
    n = b * t
    assert d == D_MODEL and t % max(ROW_TILE, R_BLOCK, Q_TILE, K_TILE) == 0
    assert K_TILE % Q_TILE == 0

    sm = jax.nn.softmax(hgrn_lb_param.astype(F32), axis=0)
    lower_bounds = jnp.cumsum(sm, axis=0) - sm[0]
    cos, s1, s2 = _rope_tables(t)
    c_g = 4 * M_WIDTH
    c_a = c_g + 8
    c_h = c_a + 3 * A_WIDTH
    head_tile = lambda g, width: jnp.tile(g.astype(F32), width // HEAD_DIM).reshape(1, width)

    x2 = x.reshape(n, d)
    for l in range(DEPTH):
        lambda_init = 0.8 - 0.6 * math.exp(-0.3 * l)
        w = w_in[l]
        cols = lambda lo, hi: w[:, lo:hi].astype(BF16)
        wg = jnp.pad(w[:, c_g:c_a], ((0, 0), (0, LANES - 8))).astype(BF16)
        wgt = jnp.pad(w[:, c_g:c_a].T, ((0, 8), (0, 0))).astype(BF16)
        gate_b = mlstm_gate_b[l].astype(F32).reshape(1, 8)
        bg = jnp.pad(gate_b, ((0, 0), (0, LANES - 8)))
        bgt = jnp.broadcast_to(jnp.pad(gate_b.reshape(8, 1), ((0, 8), (0, 0))), (16, LANES))
        wh = jnp.concatenate([w[:, c_h:c_h + H_WIDTH], w[:, c_h + 2 * H_WIDTH:]], axis=1).astype(BF16)
        um, qh, kh, vh, uh, uf, gc, gt = _inproj(
            x2, norm1_g[l].reshape(1, d), cols(0, c_g),
            cols(c_a, c_a + A_WIDTH), cols(c_a + A_WIDTH, c_a + 2 * A_WIDTH), cols(c_a + 2 * A_WIDTH, c_h),
            wh, cols(c_h + H_WIDTH, c_h + 2 * H_WIDTH), wg, wgt, bg, bgt,
            head_tile(diff_q_norm_g[l], A_WIDTH), head_tile(diff_k_norm_g[l], A_WIDTH), cos, s1, s2)

        ym, yh = _recurrent(um.reshape(b, t, -1), gc.reshape(b, t, LANES), gt, mlstm_conv_w[l].astype(F32),
                            head_tile(mlstm_norm_g[l], LANES), uh.reshape(b, t, -1), uf.reshape(b, t, -1),
                            lower_bounds[l].reshape(1, H_WIDTH), head_tile(hgrn_norm_g[l], H_WIDTH))

        lv = diff_lambda[l].astype(F32)
        lam = jnp.exp(jnp.sum(lv[0] * lv[1])) - jnp.exp(jnp.sum(lv[2] * lv[3])) + lambda_init
        ya = _attn(lam.reshape(1), qh.reshape(b, t, -1), kh.reshape(b, t, -1), vh.reshape(b, t, -1),
                   diff_subln_g[l].astype(F32).reshape(1, LANES), 1.0 - lambda_init)

        x2 = _post(x2, ym.reshape(n, -1), ya.reshape(n, -1), yh.reshape(n, -1), w_out[l].astype(BF16),
                   norm2_g[l].reshape(1, d), w_up[l].astype(BF16), w_down[l].astype(BF16))
    return x2.reshape(b, t, d)
```

```python
import functools
import math

import jax
import jax.numpy as jnp
from jax import lax
from jax.experimental import pallas as pl
from jax.experimental.pallas import tpu as pltpu

F32 = jnp.float32
BF16 = jnp.bfloat16

D_MODEL = 1024
DEPTH = 2
D_FF = 4 * D_MODEL
EPS = 1e-6
NEG_BIG = -1e30
HEAD_DIM = 64
M_WIDTH = 256
M_CONV = 4
A_WIDTH = 512
A_HEADS = 4
ROT_DIM = 16
ROPE_THETA = 500000.0
H_WIDTH = 256
LANES = 128

VMEM_LIMIT = 56 * 1024 * 1024

ROW_TILE = 512
COL_CHUNK = 512
M_CHUNK = 128
R_BLOCK = 256
H_SUB = 16
BLOCKS_PER_HEAD = 2
Q_TILE = 256
K_TILE = 512


def _nt_dot(a, b):
    return lax.dot_general(a, b, (((1,), (1,)), ((), ())), preferred_element_type=F32)


def _dot(a, b):
    return jnp.dot(a, b, preferred_element_type=F32)


def _block_ones(n, group):
    r = lax.broadcasted_iota(jnp.int32, (n, n), 0) // group
    c = lax.broadcasted_iota(jnp.int32, (n, n), 1) // group
    return jnp.where(r == c, 1.0, 0.0).astype(BF16)


def _group_mean(xsq, ones_bd, group):
    return _dot(xsq.astype(BF16), ones_bd) * (1.0 / group)


def _log_sigmoid(x):
    return jnp.minimum(x, 0.0) - jnp.log1p(jnp.exp(-jnp.abs(x)))


def _inproj_kernel(x_ref, g_ref, wm_ref, wq_ref, wk_ref, wv_ref, wh_ref, wf_ref, wg_ref, wgt_ref, bg_ref, bgt_ref,
                   qg_ref, kg_ref, cos_ref, s1_ref, s2_ref,
                   um_ref, q_ref, k_ref, v_ref, uh_ref, uf_ref, gc_ref, gt_ref):
    x = x_ref[...]
    ms = jnp.mean(x * x, axis=-1, keepdims=True)
    h = (x * lax.rsqrt(ms + EPS) * g_ref[...]).astype(BF16)

    def project(w_ref, o_ref):
        width = w_ref.shape[1]
        for c0 in range(0, width, COL_CHUNK):
            sl = slice(c0, min(c0 + COL_CHUNK, width))
            o_ref[:, sl] = _dot(h, w_ref[:, sl]).astype(o_ref.dtype)

    project(wm_ref, um_ref)
    project(wv_ref, v_ref)
    project(wh_ref, uh_ref)
    project(wf_ref, uf_ref)
    gc_ref[...] = _dot(h, wg_ref[...]) + bg_ref[...]
    gt_ref[...] = _nt_dot(wgt_ref[...], h) + bgt_ref[:, 0:1]

    ones_bd = _block_ones(LANES, HEAD_DIM)
    reps = A_WIDTH // LANES
    cos = jnp.concatenate([cos_ref[...]] * reps, axis=1)
    s1 = jnp.concatenate([s1_ref[...]] * reps, axis=1)
    s2 = jnp.concatenate([s2_ref[...]] * reps, axis=1)

    def norm_rope(u, gain):
        ms_g = jnp.concatenate(
            [_group_mean(u[:, i * LANES:(i + 1) * LANES] * u[:, i * LANES:(i + 1) * LANES], ones_bd, HEAD_DIM)
             for i in range(reps)], axis=1)
        un = u * lax.rsqrt(ms_g + EPS) * gain
        return un * cos + pltpu.roll(un, A_WIDTH - ROT_DIM // 2, 1) * s1 + pltpu.roll(un, ROT_DIM // 2, 1) * s2

    q = norm_rope(_dot(h, wq_ref[...]), qg_ref[...]) * (HEAD_DIM ** -0.5 * math.log2(math.e))
    q_ref[...] = q.astype(BF16)
    k_ref[...] = norm_rope(_dot(h, wk_ref[...]), kg_ref[...]).astype(BF16)


def _inproj(x2, g, wm, wq, wk, wv, wh, wf, wg, wgt, bg, bgt, qg, kg, cos, s1, s2):
    n = x2.shape[0]
    t_blocks = cos.shape[0] // ROW_TILE
    const = lambda a: pl.BlockSpec(a.shape, lambda i: (0, 0))
    rows = lambda width: pl.BlockSpec((ROW_TILE, width), lambda i: (i, 0))
    tab = pl.BlockSpec((ROW_TILE, LANES), lambda i: (i % t_blocks, 0))
    out = lambda width, dtype: jax.ShapeDtypeStruct((n, width), dtype)
    return pl.pallas_call(
        _inproj_kernel,
        grid=(n // ROW_TILE,),
        in_specs=[rows(D_MODEL)] + [const(a) for a in (g, wm, wq, wk, wv, wh, wf, wg, wgt, bg, bgt, qg, kg)]
                 + [tab, tab, tab],
        out_specs=[rows(wm.shape[1]), rows(A_WIDTH), rows(A_WIDTH), rows(A_WIDTH), rows(wh.shape[1]),
                   rows(wf.shape[1]), rows(LANES), pl.BlockSpec((16, ROW_TILE), lambda i: (0, i))],
        out_shape=[out(wm.shape[1], BF16), out(A_WIDTH, BF16), out(A_WIDTH, BF16), out(A_WIDTH, BF16),
                   out(wh.shape[1], BF16), out(wf.shape[1], F32), out(LANES, F32),
                   jax.ShapeDtypeStruct((16, n), F32)],
        compiler_params=pltpu.CompilerParams(dimension_semantics=("arbitrary",), vmem_limit_bytes=VMEM_LIMIT),
        name="inproj",
    )(x2, g, wm, wq, wk, wv, wh, wf, wg, wgt, bg, bgt, qg, kg, cos, s1, s2)


def _post_kernel(x_ref, ym_ref, ya_ref, yh_ref, wo_ref, g2_ref, wu_ref, wd_ref, o_ref):
    x1 = x_ref[...]
    x1 = x1 + _dot(ym_ref[...], wo_ref[0:M_WIDTH, :])
    x1 = x1 + _dot(ya_ref[...], wo_ref[M_WIDTH:M_WIDTH + A_WIDTH, :])
    x1 = x1 + _dot(yh_ref[...], wo_ref[M_WIDTH + A_WIDTH:, :])
    ms = jnp.mean(x1 * x1, axis=-1, keepdims=True)
    h2 = (x1 * lax.rsqrt(ms + EPS) * g2_ref[...]).astype(BF16)
    o_ref[...] = x1
    for c in range(D_FF // COL_CHUNK):
        sl = slice(c * COL_CHUNK, (c + 1) * COL_CHUNK)
        a = jnp.maximum(_dot(h2, wu_ref[:, sl]), 0.0)
        o_ref[...] += _dot((a * a).astype(BF16), wd_ref[sl, :])


def _post(x2, ym, ya, yh, wo, g2, wu, wd):
    n = x2.shape[0]
    const = lambda shape: pl.BlockSpec(shape, lambda i: (0, 0), pipeline_mode=pl.Buffered(1))
    rows = lambda width: pl.BlockSpec((ROW_TILE, width), lambda i: (i, 0))
    return pl.pallas_call(
        _post_kernel,
        grid=(n // ROW_TILE,),
        in_specs=[rows(D_MODEL), rows(M_WIDTH), rows(A_WIDTH), rows(H_WIDTH),
                  const(wo.shape), const((1, D_MODEL)), const(wu.shape), const(wd.shape)],
        out_specs=rows(D_MODEL),
        out_shape=jax.ShapeDtypeStruct((n, D_MODEL), F32),
        compiler_params=pltpu.CompilerParams(dimension_semantics=("arbitrary",), vmem_limit_bytes=VMEM_LIMIT),
        name="post",
    )(x2, ym, ya, yh, wo, g2, wu, wd)


def _mlstm_stream(um_ref, gc_ref, gt_ref, cw_ref, ng_ref, y_ref, tail_ref, gn_ref, m_ref):
    L = M_CHUNK
    R = R_BLOCK

    xin = um_ref[:, 0:2 * M_WIDTH].astype(F32)
    x3 = xin.reshape(R // 8, 8, 2 * M_WIDTH)
    tail = tail_ref[...].reshape(1, 8, 2 * M_WIDTH)
    sub = lax.broadcasted_iota(jnp.int32, x3.shape, 1)
    conv = None
    for j in range(M_CONV - 1):
        back = M_CONV - 1 - j
        rolled = pltpu.roll(x3, back, 1)
        prev = jnp.concatenate([pltpu.roll(tail, back, 1), rolled[:-1]], axis=0)
        term = jnp.where(sub >= back, rolled, prev) * cw_ref[j:j + 1, :]
        conv = term if conv is None else conv + term
    conv = (conv + x3 * cw_ref[M_CONV - 1:M_CONV, :]).reshape(R, 2 * M_WIDTH)
    tail_ref[...] = xin[R - 8:R, :]
    qk = conv * jax.nn.sigmoid(conv)

    gcol_all = gc_ref[...]
    grow_all = gt_ref[...]
    logf_col_all = _log_sigmoid(gcol_all)
    logf_row_all = _log_sigmoid(grow_all)

    ti = lax.broadcasted_iota(jnp.int32, (L, L), 0)
    si = lax.broadcasted_iota(jnp.int32, (L, L), 1)
    causal = si <= ti
    lane = lax.broadcasted_iota(jnp.int32, (L, LANES), 1)
    left = lane < HEAD_DIM
    ones_bd = _block_ones(LANES, HEAD_DIM)
    gr = lax.broadcasted_iota(jnp.int32, (LANES, 2 * LANES), 0)
    gc_i = lax.broadcasted_iota(jnp.int32, (LANES, 2 * LANES), 1)
    g_rowleft = gr < HEAD_DIM
    g_block = g_rowleft == ((gc_i % LANES) < HEAD_DIM)
    yield

    for ci in range(R // L):
        rows = slice(ci * L, (ci + 1) * L)
        gcol, grow = gcol_all[rows], grow_all[:, rows]
        logf_col, logf_row = logf_col_all[rows], logf_row_all[:, rows]
        for p in range(2):
            sl = slice(p * LANES, (p + 1) * LANES)
            q_pair = qk[rows, p * LANES:(p + 1) * LANES]
            k_pair = qk[rows, M_WIDTH + p * LANES:M_WIDTH + (p + 1) * LANES] * (HEAD_DIM ** -0.5)
            v_pair = um_ref[rows, 2 * M_WIDTH + p * LANES:2 * M_WIDTH + (p + 1) * LANES]
            o_pair = um_ref[rows, 3 * M_WIDTH + p * LANES:3 * M_WIDTH + (p + 1) * LANES].astype(F32)
            v1 = jnp.concatenate([v_pair, jnp.ones((L, LANES), v_pair.dtype)], axis=1)
            k_bf = k_pair.astype(BF16)
            intra, scale, einv, wcol, decay = [], [], [], [], []
            for hh in range(2):
                h = 2 * p + hh
                m_prev = m_ref[h:h + 1, 0:1]
                i_row = grow[h:h + 1, :]
                i_col = gcol[:, h:h + 1]
                lf_row = logf_row[4 + h:5 + h, :]
                lf_col = logf_col[:, 4 + h:5 + h]
                b_col = jnp.sum(jnp.where(causal, lf_row, 0.0), axis=1, keepdims=True)
                b_row = jnp.sum(jnp.where(ti <= si, lf_col, 0.0), axis=0, keepdims=True)
                a_row = i_row - b_row
                a_col = i_col - b_col
                dm = jnp.where(causal, a_row, NEG_BIG)
                m_col = jnp.maximum(jnp.max(dm, axis=1, keepdims=True), m_prev)
                e = jnp.exp(dm - m_col)
                qm = jnp.where(left if hh == 0 else jnp.logical_not(left), q_pair, 0.0).astype(BF16)
                s = _nt_dot(qm, k_bf) * e
                intra.append(_dot(s.astype(BF16), v1))
                m_last = m_col[L - 1:L, :]
                scale.append(jnp.exp(m_prev - m_col))
                einv.append(jnp.exp(-(b_col + m_col)))
                wcol.append(jnp.exp(a_col - m_last))
                decay.append(jnp.exp(m_prev - m_last))
                m_ref[h:h + 1, :] = jnp.broadcast_to(b_col[L - 1:L, :] + m_last, (1, LANES))

            gn = gn_ref[p]
            inter = _dot(q_pair.astype(BF16), gn.astype(BF16))
            sc = jnp.where(left, scale[0], scale[1])
            num = jnp.where(left, intra[0][:, :LANES], intra[1][:, :LANES]) + sc * inter[:, :LANES]
            den = jnp.where(left, intra[0][:, LANES:], intra[1][:, LANES:]) + sc * inter[:, LANES:]
            hv = num / jnp.maximum(jnp.abs(den), jnp.where(left, einv[0], einv[1]))

            kw = k_pair * jnp.where(left, wcol[0], wcol[1])
            upd = _dot(kw.T.astype(BF16), v1)
            gn_ref[p] = jnp.where(g_rowleft, decay[0], decay[1]) * gn + jnp.where(g_block, upd, 0.0)

            ms = _group_mean(hv * hv, ones_bd, HEAD_DIM)
            y = jax.nn.sigmoid(o_pair) * (hv * lax.rsqrt(ms + EPS) * ng_ref[...])
            y_ref[rows, sl] = y.astype(y_ref.dtype)
            yield


def _hgrn_stream(uh_ref, uf_ref, lb_ref, ng_ref, y_ref, w_ref, st_ref):
    R = R_BLOCK
    NS = R // H_SUB

    lb = lb_ref[...]
    fp = uf_ref[...]
    hq = uh_ref[:, 0:H_WIDTH].astype(F32)
    q = hq * jax.nn.sigmoid(hq)
    logf = jnp.log(lb + (1.0 - lb) * jax.nn.sigmoid(fp)) * math.log2(math.e)
    k = (1.0 - lb) * jax.nn.sigmoid(-fp)
    v = uh_ref[:, H_WIDTH:2 * H_WIDTH].astype(F32)

    G = R // H_SUB
    HALF = H_SUB // 2

    def halves(x):
        x4 = x.reshape(G, 2, HALF, H_WIDTH)
        return x4[:, 0], x4[:, 1]

    def whole(lo, hi):
        return jnp.stack([lo, hi], axis=1).reshape(R, H_WIDTH)

    sub = lax.broadcasted_iota(jnp.int32, (G, HALF, H_WIDTH), 1)

    def cumsum_half(x):
        shift = 1
        while shift < HALF:
            x = x + jnp.where(sub >= shift, pltpu.roll(x, shift, 1), 0.0)
            shift *= 2
        return x

    lf_lo, lf_hi = halves(logf)
    bc_lo = cumsum_half(lf_lo)
    bc_hi = cumsum_half(lf_hi) + bc_lo[:, HALF - 1:HALF, :]
    bc = whole(bc_lo, bc_hi)
    q_lo, q_hi = halves(q)
    k_lo, k_hi = halves(k)
    v_lo, v_hi = halves(v)
    ones_bd = _block_ones(LANES, HEAD_DIM)

    bl_rows = jnp.concatenate(
        [jnp.broadcast_to(bc[(j + 1) * H_SUB - 1:(j + 1) * H_SUB, :], (H_SUB, H_WIDTH)) for j in range(NS)], axis=0)
    qe = q * jnp.exp2(bc)
    ke = k * jnp.exp2(bl_rows - bc)
    tcol = lax.broadcasted_iota(jnp.int32, (LANES, R), 1) // H_SUB
    trow = lax.broadcasted_iota(jnp.int32, (R, LANES), 0) // H_SUB
    sr = lax.broadcasted_iota(jnp.int32, (LANES, LANES), 0) < HEAD_DIM
    s_block = sr == (lax.broadcasted_iota(jnp.int32, (LANES, LANES), 1) < HEAD_DIM)
    yield

    outs = []

    def history(p):
        sl = slice(p * LANES, (p + 1) * LANES)
        v_t = v[:, sl].T
        v_stack = jnp.concatenate([jnp.where(tcol == j, v_t, 0.0) for j in range(NS)], axis=0).astype(BF16)
        upd = _dot(v_stack, ke[:, sl].astype(BF16))
        yield
        st = st_ref[p]
        for j in range(NS):
            w_ref[p, :, j * LANES:(j + 1) * LANES] = st
            a_row = jnp.exp2(bc[(j + 1) * H_SUB - 1:(j + 1) * H_SUB, sl])
            st = st * a_row + jnp.where(s_block, upd[j * LANES:(j + 1) * LANES, :], 0.0)
        st_ref[p] = st
        q_exp = jnp.concatenate([jnp.where(trow == j, qe[:, sl], 0.0) for j in range(NS)], axis=1).astype(BF16)
        outs.append(_nt_dot(q_exp, w_ref[p].astype(BF16)))
        yield

    def weights(qq, bq, bs, ks, valid):
        dlt = bq - bs
        if valid is not None:
            dlt = jnp.where(valid, dlt, NEG_BIG)
        return (qq * ks * jnp.exp2(dlt)).reshape(G * HALF, H_WIDTH)

    acc = [jnp.zeros((G, HALF, H_WIDTH), F32), jnp.zeros((G, HALF, H_WIDTH), F32)]

    def direct():
        for b in range(HALF):
            if b == 0:
                valid = None
                bs_l, ks_l, vs_l = bc_lo, k_lo, v_lo
                bs_h, ks_h, vs_h = bc_hi, k_hi, v_hi
            else:
                valid = sub >= b
                bs_l, ks_l, vs_l = (pltpu.roll(x, b, 1) for x in (bc_lo, k_lo, v_lo))
                bs_h, ks_h, vs_h = (jnp.where(valid, pltpu.roll(x, b, 1), y)
                                    for x, y in ((bc_hi, bs_l), (k_hi, ks_l), (v_hi, vs_l)))
            pr = jnp.concatenate([weights(q_lo, bc_lo, bs_l, ks_l, valid),
                                  weights(q_hi, bc_hi, bs_h, ks_h, None),
                                  weights(q_hi, bc_hi, bs_l, ks_l, valid)], axis=0).astype(BF16)
            red = jnp.concatenate([_dot(pr[:, 0:LANES], ones_bd), _dot(pr[:, LANES:], ones_bd)], axis=1)
            red = red.reshape(3, G, HALF, H_WIDTH)
            acc[0] = acc[0] + red[0] * vs_l
            acc[1] = acc[1] + red[1] * vs_h + red[2] * vs_l
            yield

    hist = [history(0), history(1)]
    dire = direct()
    for piece in range(HALF):
        if piece % 2 == 0:
            next(hist[piece // 4])
            yield
        next(dire)
        yield
    o = whole(acc[0], acc[1]) + jnp.concatenate(outs, axis=1)

    ms = jnp.concatenate([_group_mean(o[:, 0:LANES] * o[:, 0:LANES], ones_bd, HEAD_DIM),
                          _group_mean(o[:, LANES:] * o[:, LANES:], ones_bd, HEAD_DIM)], axis=1)
    hg = uh_ref[:, 2 * H_WIDTH:3 * H_WIDTH].astype(F32)
    y = (o * lax.rsqrt(ms + EPS) * ng_ref[...]) * (hg * jax.nn.sigmoid(hg))
    y_ref[...] = y.astype(y_ref.dtype)
    yield


def _round_robin(streams, turns):
    live = list(streams)
    while live:
        for g, n in zip(list(streams), turns):
            for _ in range(n):
                if g in live and next(g, StopIteration) is StopIteration:
                    live.remove(g)


def _mixers_kernel(tab_ref, lam_ref, q_ref, k_ref, v_ref, sg_ref,
                   um_ref, gc_ref, gt_ref, cw_ref, mg_ref, uh_ref, uf_ref, lb_ref, hg_ref,
                   ya_ref, ym_ref, yh_ref,
                   qt_ref, vt_ref, d_ref, s0_ref, s1_ref, p0_ref, p1_ref, a0_ref, a1_ref, m_ref, acc_ref,
                   tail_ref, gn_ref, mm_ref, w_ref, st_ref, *, out_scale, diagonal):
    @pl.when(pl.program_id(1) == 0)
    def _():
        tail_ref[...] = jnp.zeros(tail_ref.shape, F32)
        gn_ref[...] = jnp.zeros(gn_ref.shape, F32)
        mm_ref[...] = jnp.zeros(mm_ref.shape, F32)
        st_ref[...] = jnp.zeros(st_ref.shape, F32)

    n_q = q_ref.shape[1] // Q_TILE
    lane = lax.broadcasted_iota(jnp.int32, (Q_TILE, LANES), 1)
    for i in range(n_q):
        q = q_ref[0, i * Q_TILE:(i + 1) * Q_TILE, :].astype(F32)
        qt_ref[i, :, 0:Q_TILE] = jnp.where(lane < HEAD_DIM, q, 0.0).T.astype(BF16)
        qt_ref[i, :, Q_TILE:] = jnp.where(lane >= HEAD_DIM, q, 0.0).T.astype(BF16)
    for j in range(vt_ref.shape[0]):
        vt_ref[j, 0:LANES, :] = v_ref[0, j * K_TILE:(j + 1) * K_TILE, :].astype(F32).T.astype(BF16)
        vt_ref[j, LANES:, :] = jnp.ones((LANES, K_TILE), BF16)
    d_ref[...] = (lax.broadcasted_iota(jnp.int32, (K_TILE, Q_TILE), 1)
                  - lax.broadcasted_iota(jnp.int32, (K_TILE, Q_TILE), 0))

    halves = (slice(0, Q_TILE), slice(Q_TILE, 2 * Q_TILE))

    def scores(t, s_ref, cs):
        kstart = pl.multiple_of(tab_ref[1, t] * K_TILE, K_TILE)
        s_ref[:, cs] = _dot(k_ref[0, pl.ds(kstart, K_TILE), :], qt_ref[tab_ref[0, t], :, cs])

    def numerators(t, s_ref, p_ref, a_ref, cs, diag):
        s = s_ref[:, cs]
        if diag:
            s = jnp.where(d_ref[...] >= tab_ref[1, t] * K_TILE - tab_ref[0, t] * Q_TILE, s, NEG_BIG)
        m_old = m_ref[:, cs]
        m_new = jnp.maximum(m_old, jnp.max(s, axis=0, keepdims=True))
        p_ref[:, cs] = jnp.exp2(s - m_new).astype(BF16)
        a_ref[:, cs] = jnp.exp2(m_old - m_new)
        m_ref[:, cs] = jnp.full(m_new.shape, NEG_BIG, F32) if diag else m_new

    def values(t, p_ref, a_ref, cs):
        acc = a_ref[:, cs] * acc_ref[:, cs] + _dot(vt_ref[tab_ref[1, t]], p_ref[:, cs])
        acc_ref[:, cs] = acc
        return acc

    def finish(t, acc0, acc1):
        on0 = acc0[0:LANES, :] / acc0[LANES:LANES + 1, :]
        on1 = acc1[0:LANES, :] / acc1[LANES:LANES + 1, :]
        o_t = on0 - lam_ref[0] * on1
        ms = jnp.mean(o_t * o_t, axis=0, keepdims=True)
        y = ((o_t * lax.rsqrt(ms + EPS)).T * sg_ref[...]) * out_scale
        ya_ref[0, pl.ds(pl.multiple_of(tab_ref[0, t] * Q_TILE, Q_TILE), Q_TILE), :] = y.astype(ya_ref.dtype)

    m_ref[...] = jnp.full(m_ref.shape, NEG_BIG, F32)
    acc_ref[...] = jnp.zeros(acc_ref.shape, F32)
    steps_per_block = len(diagonal)
    for cs in halves:
        scores(0, s0_ref, cs)
        scores(1, s1_ref, cs)
        numerators(0, s0_ref, p0_ref, a0_ref, cs, diagonal[0])

    bufs = ((s0_ref, p0_ref, a0_ref), (s1_ref, p1_ref, a1_ref))

    def attention_stream(i):
        for u in range(steps_per_block):
            t = steps_per_block * i + u
            cur, nxt = bufs[u % 2], bufs[(u + 1) % 2]
            accs = []
            for cs in halves:
                scores(t + 2, cur[0], cs)
                numerators(t + 1, *nxt, cs, diagonal[(u + 1) % steps_per_block])
                accs.append(values(t, cur[1], cur[2], cs))
                yield
            if diagonal[u]:
                finish(t, *accs)

    def body(i, carry):
        r0 = pl.multiple_of(i * R_BLOCK, R_BLOCK)
        rows = pl.ds(r0, R_BLOCK)
        mlstm = _mlstm_stream(um_ref.at[0, rows], gc_ref.at[0, rows], gt_ref.at[i], cw_ref, mg_ref,
                              ym_ref.at[0, rows], tail_ref, gn_ref, mm_ref)
        hgrn = _hgrn_stream(uh_ref.at[0, rows], uf_ref.at[0, rows], lb_ref, hg_ref, yh_ref.at[0, rows],
                            w_ref, st_ref)
        _round_robin((attention_stream(i), hgrn, mlstm), (4, 3, 1))
        return carry

    lax.fori_loop(0, BLOCKS_PER_HEAD, body, 0)


def _attn_schedule(t):
    n_q = t // Q_TILE
    groups, flags = [], []
    for g in range(BLOCKS_PER_HEAD):
        cols, last = [], []
        for qi in range(g, n_q, BLOCKS_PER_HEAD):
            n_kv = (qi * Q_TILE) // K_TILE + 1
            cols += [(qi, kj) for kj in range(n_kv)]
            last += [kj == n_kv - 1 for kj in range(n_kv)]
        groups.append(cols)
        flags.append(tuple(last))
    assert all(f == flags[0] for f in flags) and len(flags[0]) % 2 == 0
    cols = sum(groups, []) + [(0, 0)] * 2
    return flags[0], jnp.asarray(cols, jnp.int32).T


def _mixers(lam, qh, kh, vh, sg, out_scale, um, gc, gt, conv_w, mg, uh, uf, lb, hg):
    b, t, _ = qh.shape
    assert t == A_HEADS * BLOCKS_PER_HEAD * R_BLOCK
    diagonal, tab = _attn_schedule(t)
    rows_h = BLOCKS_PER_HEAD * R_BLOCK
    nb = t // R_BLOCK
    gt3 = gt.reshape(16, b * nb, R_BLOCK).transpose(1, 0, 2)
    seq = pl.BlockSpec((1, t, LANES), lambda i, h: (i, 0, h))
    smem = pl.BlockSpec(memory_space=pltpu.SMEM)
    blk = lambda width: pl.BlockSpec((1, rows_h, width), lambda i, h: (i, h, 0))
    const = lambda a: pl.BlockSpec(a.shape, lambda i, h: (0, 0))
    cols = 2 * Q_TILE
    ns = R_BLOCK // H_SUB
    return pl.pallas_call(
        functools.partial(_mixers_kernel, out_scale=out_scale, diagonal=diagonal),
        grid=(b, A_HEADS),
        in_specs=[smem, smem, seq, seq, seq, const(sg),
                  blk(4 * M_WIDTH), blk(LANES),
                  pl.BlockSpec((BLOCKS_PER_HEAD, 16, R_BLOCK), lambda i, h: (i * A_HEADS + h, 0, 0)),
                  const(conv_w), const(mg), blk(3 * H_WIDTH), blk(H_WIDTH), const(lb), const(hg)],
        out_specs=[seq, blk(M_WIDTH), blk(H_WIDTH)],
        out_shape=[jax.ShapeDtypeStruct((b, t, A_WIDTH), BF16), jax.ShapeDtypeStruct((b, t, M_WIDTH), BF16),
                   jax.ShapeDtypeStruct((b, t, H_WIDTH), BF16)],
        scratch_shapes=[pltpu.VMEM((t // Q_TILE, LANES, cols), BF16),
                        pltpu.VMEM((t // K_TILE, 2 * LANES, K_TILE), BF16),
                        pltpu.VMEM((K_TILE, Q_TILE), jnp.int32),
                        pltpu.VMEM((K_TILE, cols), F32), pltpu.VMEM((K_TILE, cols), F32),
                        pltpu.VMEM((K_TILE, cols), BF16), pltpu.VMEM((K_TILE, cols), BF16),
                        pltpu.VMEM((1, cols), F32), pltpu.VMEM((1, cols), F32),
                        pltpu.VMEM((1, cols), F32),
                        pltpu.VMEM((2 * LANES, cols), F32),
                        pltpu.VMEM((8, 2 * M_WIDTH), F32),
                        pltpu.VMEM((2, LANES, 2 * LANES), F32),
                        pltpu.VMEM((8, LANES), F32),
                        pltpu.VMEM((2, LANES, ns * LANES), F32),
                        pltpu.VMEM((2, LANES, LANES), F32)],
        compiler_params=pltpu.CompilerParams(dimension_semantics=("arbitrary", "arbitrary"),
                                             vmem_limit_bytes=VMEM_LIMIT),
        name="mixers",
    )(tab, lam, qh, kh, vh, sg, um, gc, gt3, conv_w, mg, uh, uf, lb, hg)


def _rope_tables(t):
    half = ROT_DIM // 2
    inv = ROPE_THETA ** (-jnp.arange(half, dtype=F32) / half)
    ang = jnp.arange(t, dtype=F32)[:, None] * inv[None, :]
    cos, sin = jnp.cos(ang), jnp.sin(ang)
    pad = jnp.zeros((t, HEAD_DIM - ROT_DIM), F32)
    zero = jnp.zeros((t, half), F32)
    c = jnp.concatenate([cos, cos, pad + 1.0], axis=1)
    s1 = jnp.concatenate([-sin, zero, pad], axis=1)
    s2 = jnp.concatenate([zero, sin, pad], axis=1)
    tile = lambda a: jnp.concatenate([a, a], axis=1)
    return tile(c), tile(s1), tile(s2)


def kernel(x, norm1_g, w_in, mlstm_conv_w, mlstm_gate_b, mlstm_norm_g, diff_q_norm_g, diff_k_norm_g,
           diff_lambda, diff_subln_g, hgrn_lb_param, hgrn_norm_g, w_out, norm2_g, w_up, w_down):
    b, t, d = x.shape
    n = b * t
    assert d == D_MODEL and t % max(ROW_TILE, R_BLOCK, Q_TILE, K_TILE) == 0
    assert K_TILE % Q_TILE == 0

    sm = jax.nn.softmax(hgrn_lb_param.astype(F32), axis=0)
    lower_bounds = jnp.cumsum(sm, axis=0) - sm[0]
    cos, s1, s2 = _rope_tables(t)
    c_g = 4 * M_WIDTH
    c_a = c_g + 8
    c_h = c_a + 3 * A_WIDTH
    head_tile = lambda g, width: jnp.tile(g.astype(F32), width // HEAD_DIM).reshape(1, width)

    x2 = x.reshape(n, d)
    for l in range(DEPTH):
        lambda_init = 0.8 - 0.6 * math.exp(-0.3 * l)
        w = w_in[l]
        cols = lambda lo, hi: w[:, lo:hi].astype(BF16)
        wg = jnp.pad(w[:, c_g:c_a], ((0, 0), (0, LANES - 8))).astype(BF16)
        wgt = jnp.pad(w[:, c_g:c_a].T, ((0, 8), (0, 0))).astype(BF16)
        gate_b = mlstm_gate_b[l].astype(F32).reshape(1, 8)
        bg = jnp.pad(gate_b, ((0, 0), (0, LANES - 8)))
        bgt = jnp.broadcast_to(jnp.pad(gate_b.reshape(8, 1), ((0, 8), (0, 0))), (16, LANES))
        wh = jnp.concatenate([w[:, c_h:c_h + H_WIDTH], w[:, c_h + 2 * H_WIDTH:]], axis=1).astype(BF16)
        um, qh, kh, vh, uh, uf, gc, gt = _inproj(
            x2, norm1_g[l].reshape(1, d), cols(0, c_g),
            cols(c_a, c_a + A_WIDTH), cols(c_a + A_WIDTH, c_a + 2 * A_WIDTH), cols(c_a + 2 * A_WIDTH, c_h),
            wh, cols(c_h + H_WIDTH, c_h + 2 * H_WIDTH), wg, wgt, bg, bgt,
            head_tile(diff_q_norm_g[l], A_WIDTH), head_tile(diff_k_norm_g[l], A_WIDTH), cos, s1, s2)

        lv = diff_lambda[l].astype(F32)
        lam = jnp.exp(jnp.sum(lv[0] * lv[1])) - jnp.exp(jnp.sum(lv[2] * lv[3])) + lambda_init
        ya, ym, yh = _mixers(
            lam.reshape(1), qh.reshape(b, t, -1), kh.reshape(b, t, -1), vh.reshape(b, t, -1),
            diff_subln_g[l].astype(F32).reshape(1, LANES), 1.0 - lambda_init,
            um.reshape(b, t, -1), gc.reshape(b, t, LANES), gt, mlstm_conv_w[l].astype(F32),
            head_tile(mlstm_norm_g[l], LANES), uh.reshape(b, t, -1), uf.reshape(b, t, -1),
            lower_bounds[l].reshape(1, H_WIDTH), head_tile(hgrn_norm_g[l], H_WIDTH))

        x2 = _post(x2, ym.reshape(n, -1), ya.reshape(n, -1), yh.reshape(n, -1), w_out[l].astype(BF16),
                   norm2_g[l].reshape(1, d), w_up[l].astype(BF16), w_down[l].astype(BF16))
    return x2.reshape(b, t, d)
```

```python
import functools
import math

import jax
import jax.numpy as jnp
from jax import lax
from jax.experimental import pallas as pl
from jax.experimental.pallas import tpu as pltpu

F32 = jnp.float32
BF16 = jnp.bfloat16

D_MODEL = 1024
DEPTH = 2
D_FF = 4 * D_MODEL
EPS = 1e-6
NEG_BIG = -1e30
HEAD_DIM = 64
M_WIDTH = 256
M_CONV = 4
A_WIDTH = 512
A_HEADS = 4
ROT_DIM = 16
ROPE_THETA = 500000.0
H_WIDTH = 256
LANES = 128

VMEM_LIMIT = 56 * 1024 * 1024

ROW_TILE = 512
COL_CHUNK = 512
M_CHUNK = 128
R_BLOCK = 256
H_SUB = 16
DEN_ROWS = 16
BLOCKS_PER_HEAD = 2
Q_TILE = 256
K_TILE = 512


def _nt_dot(a, b):
    return lax.dot_general(a, b, (((1,), (1,)), ((), ())), preferred_element_type=F32)


def _dot(a, b):
    return jnp.dot(a, b, preferred_element_type=F32)


def _block_ones(n, group):
    r = lax.broadcasted_iota(jnp.int32, (n, n), 0) // group
    c = lax.broadcasted_iota(jnp.int32, (n, n), 1) // group
    return jnp.where(r == c, 1.0, 0.0).astype(BF16)


def _group_mean(xsq, ones_bd, group):
    return _dot(xsq.astype(BF16), ones_bd) * (1.0 / group)


def _log_sigmoid(x):
    return jnp.minimum(x, 0.0) - jnp.log1p(jnp.exp(-jnp.abs(x)))


def _round_robin(streams, turns):
    live = list(streams)
    while live:
        for g, n in zip(list(streams), turns):
            for _ in range(n):
                if g in live and next(g, StopIteration) is StopIteration:
                    live.remove(g)


def _inproj_kernel(x_ref, g_ref, wm_ref, wq_ref, wk_ref, wv_ref, wh_ref, wf_ref, wg_ref, wgt_ref, bg_ref, bgt_ref,
                   qg_ref, kg_ref, cos_ref, s1_ref, s2_ref,
                   um_ref, q_ref, k_ref, v_ref, uh_ref, uf_ref, gc_ref, gt_ref):
    x = x_ref[...]
    ms = jnp.mean(x * x, axis=-1, keepdims=True)
    h = (x * lax.rsqrt(ms + EPS) * g_ref[...]).astype(BF16)

    ones_bd2 = _block_ones(2 * LANES, HEAD_DIM)
    cos, s1, s2 = (jnp.concatenate([r[...], r[...]], axis=1) for r in (cos_ref, s1_ref, s2_ref))

    def project(w_ref, o_ref):
        width = w_ref.shape[1]
        for c0 in range(0, width, COL_CHUNK):
            sl = slice(c0, min(c0 + COL_CHUNK, width))
            o_ref[:, sl] = _dot(h, w_ref[:, sl]).astype(o_ref.dtype)
            yield

    def plain():
        yield from project(wm_ref, um_ref)
        yield from project(wv_ref, v_ref)
        yield from project(wh_ref, uh_ref)
        yield from project(wf_ref, uf_ref)
        gc_ref[...] = _dot(h, wg_ref[...]) + bg_ref[...]
        gt_ref[...] = _nt_dot(wgt_ref[...], h) + bgt_ref[:, 0:1]
        yield

    def rotated(w_ref, gain_ref, o_ref, scale):
        u = _dot(h, w_ref[...])
        yield
        for i in range(A_WIDTH // (2 * LANES)):
            sl = slice(2 * i * LANES, 2 * (i + 1) * LANES)
            us = u[:, sl]
            un = us * lax.rsqrt(_group_mean(us * us, ones_bd2, HEAD_DIM) + EPS) * gain_ref[:, sl]
            rot = (un * cos + pltpu.roll(un, 2 * LANES - ROT_DIM // 2, 1) * s1
                   + pltpu.roll(un, ROT_DIM // 2, 1) * s2)
            o_ref[:, sl] = (rot * scale).astype(BF16)
            yield

    def attention_inputs():
        yield from rotated(wq_ref, qg_ref, q_ref, HEAD_DIM ** -0.5 * math.log2(math.e))
        yield from rotated(wk_ref, kg_ref, k_ref, 1.0)

    _round_robin((attention_inputs(), plain()), (1, 1))


def _inproj(x2, g, wm, wq, wk, wv, wh, wf, wg, wgt, bg, bgt, qg, kg, cos, s1, s2):
    n = x2.shape[0]
    t_blocks = cos.shape[0] // ROW_TILE
    const = lambda a: pl.BlockSpec(a.shape, lambda i: (0, 0))
    rows = lambda width: pl.BlockSpec((ROW_TILE, width), lambda i: (i, 0))
    tab = pl.BlockSpec((ROW_TILE, LANES), lambda i: (i % t_blocks, 0))
    out = lambda width, dtype: jax.ShapeDtypeStruct((n, width), dtype)
    return pl.pallas_call(
        _inproj_kernel,
        grid=(n // ROW_TILE,),
        in_specs=[rows(D_MODEL)] + [const(a) for a in (g, wm, wq, wk, wv, wh, wf, wg, wgt, bg, bgt, qg, kg)]
                 + [tab, tab, tab],
        out_specs=[rows(wm.shape[1]), rows(A_WIDTH), rows(A_WIDTH), rows(A_WIDTH), rows(wh.shape[1]),
                   rows(wf.shape[1]), rows(LANES), pl.BlockSpec((16, ROW_TILE), lambda i: (0, i))],
        out_shape=[out(wm.shape[1], BF16), out(A_WIDTH, BF16), out(A_WIDTH, BF16), out(A_WIDTH, BF16),
                   out(wh.shape[1], BF16), out(wf.shape[1], F32), out(LANES, F32),
                   jax.ShapeDtypeStruct((16, n), F32)],
        compiler_params=pltpu.CompilerParams(dimension_semantics=("arbitrary",), vmem_limit_bytes=VMEM_LIMIT),
        name="inproj",
    )(x2, g, wm, wq, wk, wv, wh, wf, wg, wgt, bg, bgt, qg, kg, cos, s1, s2)


def _post_kernel(x_ref, ym_ref, ya_ref, yh_ref, wo_ref, g2_ref, wu_ref, wd_ref, o_ref):
    x1 = x_ref[...]
    x1 = x1 + _dot(ym_ref[...], wo_ref[0:M_WIDTH, :])
    x1 = x1 + _dot(ya_ref[...], wo_ref[M_WIDTH:M_WIDTH + A_WIDTH, :])
    x1 = x1 + _dot(yh_ref[...], wo_ref[M_WIDTH + A_WIDTH:, :])
    ms = jnp.mean(x1 * x1, axis=-1, keepdims=True)
    h2 = (x1 * lax.rsqrt(ms + EPS) * g2_ref[...]).astype(BF16)
    o_ref[...] = x1
    for c in range(D_FF // COL_CHUNK):
        sl = slice(c * COL_CHUNK, (c + 1) * COL_CHUNK)
        a = jnp.maximum(_dot(h2, wu_ref[:, sl]), 0.0)
        o_ref[...] += _dot((a * a).astype(BF16), wd_ref[sl, :])


def _post(x2, ym, ya, yh, wo, g2, wu, wd):
    n = x2.shape[0]
    const = lambda shape: pl.BlockSpec(shape, lambda i: (0, 0), pipeline_mode=pl.Buffered(1))
    rows = lambda width: pl.BlockSpec((ROW_TILE, width), lambda i: (i, 0))
    return pl.pallas_call(
        _post_kernel,
        grid=(n // ROW_TILE,),
        in_specs=[rows(D_MODEL), rows(M_WIDTH), rows(A_WIDTH), rows(H_WIDTH),
                  const(wo.shape), const((1, D_MODEL)), const(wu.shape), const(wd.shape)],
        out_specs=rows(D_MODEL),
        out_shape=jax.ShapeDtypeStruct((n, D_MODEL), F32),
        compiler_params=pltpu.CompilerParams(dimension_semantics=("arbitrary",), vmem_limit_bytes=VMEM_LIMIT),
        name="post",
    )(x2, ym, ya, yh, wo, g2, wu, wd)


def _mlstm_stream(um_ref, gc_ref, gt_ref, cw_ref, ng_ref, y_ref, tail_ref, gn_ref, m_ref):
    L = M_CHUNK
    R = R_BLOCK

    xin = um_ref[:, 0:2 * M_WIDTH].astype(F32)
    x3 = xin.reshape(R // 8, 8, 2 * M_WIDTH)
    tail = tail_ref[...].reshape(1, 8, 2 * M_WIDTH)
    sub = lax.broadcasted_iota(jnp.int32, x3.shape, 1)
    conv = None
    for j in range(M_CONV - 1):
        back = M_CONV - 1 - j
        rolled = pltpu.roll(x3, back, 1)
        prev = jnp.concatenate([pltpu.roll(tail, back, 1), rolled[:-1]], axis=0)
        term = jnp.where(sub >= back, rolled, prev) * cw_ref[j:j + 1, :]
        conv = term if conv is None else conv + term
    conv = (conv + x3 * cw_ref[M_CONV - 1:M_CONV, :]).reshape(R, 2 * M_WIDTH)
    tail_ref[...] = xin[R - 8:R, :]
    qk = conv * jax.nn.sigmoid(conv)

    gcol_all = gc_ref[...]
    grow_all = gt_ref[...]
    logf_col_all = _log_sigmoid(gcol_all)
    logf_row_all = _log_sigmoid(grow_all)

    ti = lax.broadcasted_iota(jnp.int32, (L, L), 0)
    si = lax.broadcasted_iota(jnp.int32, (L, L), 1)
    causal = si <= ti
    lane = lax.broadcasted_iota(jnp.int32, (L, LANES), 1)
    left = lane < HEAD_DIM
    ones_bd = _block_ones(LANES, HEAD_DIM)
    gr = lax.broadcasted_iota(jnp.int32, (LANES, 2 * LANES), 0)
    gc_i = lax.broadcasted_iota(jnp.int32, (LANES, 2 * LANES), 1)
    g_rowleft = gr < HEAD_DIM
    g_block = g_rowleft == ((gc_i % LANES) < HEAD_DIM)
    yield

    for ci in range(R // L):
        rows = slice(ci * L, (ci + 1) * L)
        gcol, grow = gcol_all[rows], grow_all[:, rows]
        logf_col, logf_row = logf_col_all[rows], logf_row_all[:, rows]
        for p in range(2):
            sl = slice(p * LANES, (p + 1) * LANES)
            q_pair = qk[rows, p * LANES:(p + 1) * LANES]
            k_pair = qk[rows, M_WIDTH + p * LANES:M_WIDTH + (p + 1) * LANES] * (HEAD_DIM ** -0.5)
            v_pair = um_ref[rows, 2 * M_WIDTH + p * LANES:2 * M_WIDTH + (p + 1) * LANES]
            o_pair = um_ref[rows, 3 * M_WIDTH + p * LANES:3 * M_WIDTH + (p + 1) * LANES].astype(F32)
            v1 = jnp.concatenate([v_pair, jnp.ones((L, LANES), v_pair.dtype)], axis=1)
            k_bf = k_pair.astype(BF16)
            intra, scale, einv, wcol, decay = [], [], [], [], []
            for hh in range(2):
                h = 2 * p + hh
                m_prev = m_ref[h:h + 1, 0:1]
                i_row = grow[h:h + 1, :]
                i_col = gcol[:, h:h + 1]
                lf_row = logf_row[4 + h:5 + h, :]
                lf_col = logf_col[:, 4 + h:5 + h]
                b_col = jnp.sum(jnp.where(causal, lf_row, 0.0), axis=1, keepdims=True)
                b_row = jnp.sum(jnp.where(ti <= si, lf_col, 0.0), axis=0, keepdims=True)
                a_row = i_row - b_row
                a_col = i_col - b_col
                dm = jnp.where(causal, a_row, NEG_BIG)
                m_col = jnp.maximum(jnp.max(dm, axis=1, keepdims=True), m_prev)
                e = jnp.exp(dm - m_col)
                qm = jnp.where(left if hh == 0 else jnp.logical_not(left), q_pair, 0.0).astype(BF16)
                s = _nt_dot(qm, k_bf) * e
                intra.append(_dot(s.astype(BF16), v1))
                m_last = m_col[L - 1:L, :]
                scale.append(jnp.exp(m_prev - m_col))
                einv.append(jnp.exp(-(b_col + m_col)))
                wcol.append(jnp.exp(a_col - m_last))
                decay.append(jnp.exp(m_prev - m_last))
                m_ref[h:h + 1, :] = jnp.broadcast_to(b_col[L - 1:L, :] + m_last, (1, LANES))

            gn = gn_ref[p]
            inter = _dot(q_pair.astype(BF16), gn.astype(BF16))
            sc = jnp.where(left, scale[0], scale[1])
            num = jnp.where(left, intra[0][:, :LANES], intra[1][:, :LANES]) + sc * inter[:, :LANES]
            den = jnp.where(left, intra[0][:, LANES:], intra[1][:, LANES:]) + sc * inter[:, LANES:]
            hv = num / jnp.maximum(jnp.abs(den), jnp.where(left, einv[0], einv[1]))

            kw = k_pair * jnp.where(left, wcol[0], wcol[1])
            upd = _dot(kw.T.astype(BF16), v1)
            gn_ref[p] = jnp.where(g_rowleft, decay[0], decay[1]) * gn + jnp.where(g_block, upd, 0.0)

            ms = _group_mean(hv * hv, ones_bd, HEAD_DIM)
            y = jax.nn.sigmoid(o_pair) * (hv * lax.rsqrt(ms + EPS) * ng_ref[...])
            y_ref[rows, sl] = y.astype(y_ref.dtype)
            yield


def _hgrn_stream(uh_ref, uf_ref, lb_ref, ng_ref, y_ref, w_ref, st_ref):
    R = R_BLOCK
    NS = R // H_SUB

    lb = lb_ref[...]
    fp = uf_ref[...]
    hq = uh_ref[:, 0:H_WIDTH].astype(F32)
    q = hq * jax.nn.sigmoid(hq)
    logf = jnp.log(lb + (1.0 - lb) * jax.nn.sigmoid(fp)) * math.log2(math.e)
    k = (1.0 - lb) * jax.nn.sigmoid(-fp)
    v = uh_ref[:, H_WIDTH:2 * H_WIDTH].astype(F32)

    G = R // H_SUB
    HALF = H_SUB // 2

    def halves(x):
        x4 = x.reshape(G, 2, HALF, H_WIDTH)
        return x4[:, 0], x4[:, 1]

    def whole(lo, hi):
        return jnp.stack([lo, hi], axis=1).reshape(R, H_WIDTH)

    sub = lax.broadcasted_iota(jnp.int32, (G, HALF, H_WIDTH), 1)

    def cumsum_half(x):
        shift = 1
        while shift < HALF:
            x = x + jnp.where(sub >= shift, pltpu.roll(x, shift, 1), 0.0)
            shift *= 2
        return x

    lf_lo, lf_hi = halves(logf)
    bc_lo = cumsum_half(lf_lo)
    bc_hi = cumsum_half(lf_hi) + bc_lo[:, HALF - 1:HALF, :]
    bc = whole(bc_lo, bc_hi)
    q_lo, q_hi = halves(q)
    k_lo, k_hi = halves(k)
    v_lo, v_hi = halves(v)
    ones_bd2 = _block_ones(H_WIDTH, HEAD_DIM)

    bl_rows = jnp.concatenate(
        [jnp.broadcast_to(bc[(j + 1) * H_SUB - 1:(j + 1) * H_SUB, :], (H_SUB, H_WIDTH)) for j in range(NS)], axis=0)
    qe = q * jnp.exp2(bc)
    ke = k * jnp.exp2(bl_rows - bc)
    tcol = lax.broadcasted_iota(jnp.int32, (LANES, R), 1) // H_SUB
    trow = lax.broadcasted_iota(jnp.int32, (R, LANES), 0) // H_SUB
    sr = lax.broadcasted_iota(jnp.int32, (LANES, LANES), 0) < HEAD_DIM
    s_block = sr == (lax.broadcasted_iota(jnp.int32, (LANES, LANES), 1) < HEAD_DIM)
    yield

    outs = []

    def history(p):
        sl = slice(p * LANES, (p + 1) * LANES)
        v_t = v[:, sl].T
        v_stack = jnp.concatenate([jnp.where(tcol == j, v_t, 0.0) for j in range(NS)], axis=0).astype(BF16)
        upd = _dot(v_stack, ke[:, sl].astype(BF16))
        yield
        st = st_ref[p]
        for j in range(NS):
            w_ref[p, :, j * LANES:(j + 1) * LANES] = st
            a_row = jnp.exp2(bc[(j + 1) * H_SUB - 1:(j + 1) * H_SUB, sl])
            st = st * a_row + jnp.where(s_block, upd[j * LANES:(j + 1) * LANES, :], 0.0)
        st_ref[p] = st
        q_exp = jnp.concatenate([jnp.where(trow == j, qe[:, sl], 0.0) for j in range(NS)], axis=1).astype(BF16)
        outs.append(_nt_dot(q_exp, w_ref[p].astype(BF16)))
        yield

    def weights(qq, bq, bs, ks, valid):
        dlt = bq - bs
        if valid is not None:
            dlt = jnp.where(valid, dlt, NEG_BIG)
        return (qq * ks * jnp.exp2(dlt)).reshape(G * HALF, H_WIDTH)

    acc = [jnp.zeros((G, HALF, H_WIDTH), F32), jnp.zeros((G, HALF, H_WIDTH), F32)]

    def direct():
        for b in range(HALF):
            if b == 0:
                valid = None
                bs_l, ks_l, vs_l = bc_lo, k_lo, v_lo
                bs_h, ks_h, vs_h = bc_hi, k_hi, v_hi
            else:
                valid = sub >= b
                bs_l, ks_l, vs_l = (pltpu.roll(x, b, 1) for x in (bc_lo, k_lo, v_lo))
                bs_h, ks_h, vs_h = (jnp.where(valid, pltpu.roll(x, b, 1), y)
                                    for x, y in ((bc_hi, bs_l), (k_hi, ks_l), (v_hi, vs_l)))
            pr = jnp.concatenate([weights(q_lo, bc_lo, bs_l, ks_l, valid),
                                  weights(q_hi, bc_hi, bs_h, ks_h, None),
                                  weights(q_hi, bc_hi, bs_l, ks_l, valid)], axis=0).astype(BF16)
            red = _dot(pr, ones_bd2)
            red = red.reshape(3, G, HALF, H_WIDTH)
            acc[0] = acc[0] + red[0] * vs_l
            acc[1] = acc[1] + red[1] * vs_h + red[2] * vs_l
            yield

    hist = [history(0), history(1)]
    dire = direct()
    for piece in range(HALF):
        if piece % 2 == 0:
            next(hist[piece // 4])
            yield
        next(dire)
        yield
    o = whole(acc[0], acc[1]) + jnp.concatenate(outs, axis=1)

    ms = _group_mean(o * o, ones_bd2, HEAD_DIM)
    hg = uh_ref[:, 2 * H_WIDTH:3 * H_WIDTH].astype(F32)
    y = (o * lax.rsqrt(ms + EPS) * ng_ref[...]) * (hg * jax.nn.sigmoid(hg))
    y_ref[...] = y.astype(y_ref.dtype)
    yield


def _mixers_kernel(tab_ref, lam_ref, q_ref, k_ref, v_ref, sg_ref,
                   um_ref, gc_ref, gt_ref, cw_ref, mg_ref, uh_ref, uf_ref, lb_ref, hg_ref,
                   ya_ref, ym_ref, yh_ref,
                   qt_ref, vt_ref, d_ref, s0_ref, s1_ref, p0_ref, p1_ref, a0_ref, a1_ref, m_ref, acc_ref,
                   tail_ref, gn_ref, mm_ref, w_ref, st_ref, *, out_scale, diagonal):
    @pl.when(pl.program_id(1) == 0)
    def _():
        tail_ref[...] = jnp.zeros(tail_ref.shape, F32)
        gn_ref[...] = jnp.zeros(gn_ref.shape, F32)
        mm_ref[...] = jnp.zeros(mm_ref.shape, F32)
        st_ref[...] = jnp.zeros(st_ref.shape, F32)

    n_q = q_ref.shape[1] // Q_TILE
    lane = lax.broadcasted_iota(jnp.int32, (Q_TILE, LANES), 1)
    for i in range(n_q):
        q = q_ref[0, i * Q_TILE:(i + 1) * Q_TILE, :].astype(F32)
        qt_ref[i, :, 0:Q_TILE] = jnp.where(lane < HEAD_DIM, q, 0.0).T.astype(BF16)
        qt_ref[i, :, Q_TILE:] = jnp.where(lane >= HEAD_DIM, q, 0.0).T.astype(BF16)
    for j in range(vt_ref.shape[0]):
        vt_ref[j, 0:LANES, :] = v_ref[0, j * K_TILE:(j + 1) * K_TILE, :].astype(F32).T.astype(BF16)
        vt_ref[j, LANES:, :] = jnp.ones((DEN_ROWS, K_TILE), BF16)
    d_ref[...] = (lax.broadcasted_iota(jnp.int32, (K_TILE, Q_TILE), 1)
                  - lax.broadcasted_iota(jnp.int32, (K_TILE, Q_TILE), 0))

    halves = (slice(0, Q_TILE), slice(Q_TILE, 2 * Q_TILE))

    def scores(t, s_ref, cs):
        kstart = pl.multiple_of(tab_ref[1, t] * K_TILE, K_TILE)
        s_ref[:, cs] = _dot(k_ref[0, pl.ds(kstart, K_TILE), :], qt_ref[tab_ref[0, t], :, cs])

    def numerators(t, s_ref, p_ref, a_ref, cs, diag):
        s = s_ref[:, cs]
        if diag:
            s = jnp.where(d_ref[...] >= tab_ref[1, t] * K_TILE - tab_ref[0, t] * Q_TILE, s, NEG_BIG)
        m_old = m_ref[:, cs]
        m_new = jnp.maximum(m_old, jnp.max(s, axis=0, keepdims=True))
        p_ref[:, cs] = jnp.exp2(s - m_new).astype(BF16)
        a_ref[:, cs] = jnp.exp2(m_old - m_new)
        m_ref[:, cs] = jnp.full(m_new.shape, NEG_BIG, F32) if diag else m_new

    def values(t, p_ref, a_ref, cs):
        acc = a_ref[:, cs] * acc_ref[:, cs] + _dot(vt_ref[tab_ref[1, t]], p_ref[:, cs])
        acc_ref[:, cs] = acc
        return acc

    def finish(t, acc0, acc1):
        on0 = acc0[0:LANES, :] / acc0[LANES:LANES + 1, :]
        on1 = acc1[0:LANES, :] / acc1[LANES:LANES + 1, :]
        o_t = on0 - lam_ref[0] * on1
        ms = jnp.mean(o_t * o_t, axis=0, keepdims=True)
        y = ((o_t * lax.rsqrt(ms + EPS)).T * sg_ref[...]) * out_scale
        ya_ref[0, pl.ds(pl.multiple_of(tab_ref[0, t] * Q_TILE, Q_TILE), Q_TILE), :] = y.astype(ya_ref.dtype)

    m_ref[...] = jnp.full(m_ref.shape, NEG_BIG, F32)
    acc_ref[...] = jnp.zeros(acc_ref.shape, F32)
    steps_per_block = len(diagonal)
    for cs in halves:
        scores(0, s0_ref, cs)
        scores(1, s1_ref, cs)
        numerators(0, s0_ref, p0_ref, a0_ref, cs, diagonal[0])

    bufs = ((s0_ref, p0_ref, a0_ref), (s1_ref, p1_ref, a1_ref))

    def attention_stream(i):
        for u in range(steps_per_block):
            t = steps_per_block * i + u
            cur, nxt = bufs[u % 2], bufs[(u + 1) % 2]
            accs = []
            for cs in halves:
                scores(t + 2, cur[0], cs)
                numerators(t + 1, *nxt, cs, diagonal[(u + 1) % steps_per_block])
                accs.append(values(t, cur[1], cur[2], cs))
                yield
            if diagonal[u]:
                finish(t, *accs)

    def body(i, carry):
        r0 = pl.multiple_of(i * R_BLOCK, R_BLOCK)
        rows = pl.ds(r0, R_BLOCK)
        mlstm = _mlstm_stream(um_ref.at[0, rows], gc_ref.at[0, rows], gt_ref.at[i], cw_ref, mg_ref,
                              ym_ref.at[0, rows], tail_ref, gn_ref, mm_ref)
        hgrn = _hgrn_stream(uh_ref.at[0, rows], uf_ref.at[0, rows], lb_ref, hg_ref, yh_ref.at[0, rows],
                            w_ref, st_ref)
        _round_robin((attention_stream(i), hgrn, mlstm), (4, 3, 1))
        return carry

    lax.fori_loop(0, BLOCKS_PER_HEAD, body, 0)


def _attn_schedule(t):
    n_q = t // Q_TILE
    groups, flags = [], []
    for g in range(BLOCKS_PER_HEAD):
        cols, last = [], []
        for qi in range(g, n_q, BLOCKS_PER_HEAD):
            n_kv = (qi * Q_TILE) // K_TILE + 1
            cols += [(qi, kj) for kj in range(n_kv)]
            last += [kj == n_kv - 1 for kj in range(n_kv)]
        groups.append(cols)
        flags.append(tuple(last))
    assert all(f == flags[0] for f in flags) and len(flags[0]) % 2 == 0
    cols = sum(groups, []) + [(0, 0)] * 2
    return flags[0], jnp.asarray(cols, jnp.int32).T


def _mixers(lam, qh, kh, vh, sg, out_scale, um, gc, gt, conv_w, mg, uh, uf, lb, hg):
    b, t, _ = qh.shape
    assert t == A_HEADS * BLOCKS_PER_HEAD * R_BLOCK
    diagonal, tab = _attn_schedule(t)
    rows_h = BLOCKS_PER_HEAD * R_BLOCK
    nb = t // R_BLOCK
    gt3 = gt.reshape(16, b * nb, R_BLOCK).transpose(1, 0, 2)
    seq = pl.BlockSpec((1, t, LANES), lambda i, h: (i, 0, h))
    smem = pl.BlockSpec(memory_space=pltpu.SMEM)
    blk = lambda width: pl.BlockSpec((1, rows_h, width), lambda i, h: (i, h, 0))
    const = lambda a: pl.BlockSpec(a.shape, lambda i, h: (0, 0))
    cols = 2 * Q_TILE
    ns = R_BLOCK // H_SUB
    return pl.pallas_call(
        functools.partial(_mixers_kernel, out_scale=out_scale, diagonal=diagonal),
        grid=(b, A_HEADS),
        in_specs=[smem, smem, seq, seq, seq, const(sg),
                  blk(4 * M_WIDTH), blk(LANES),
                  pl.BlockSpec((BLOCKS_PER_HEAD, 16, R_BLOCK), lambda i, h: (i * A_HEADS + h, 0, 0)),
                  const(conv_w), const(mg), blk(3 * H_WIDTH), blk(H_WIDTH), const(lb), const(hg)],
        out_specs=[seq, blk(M_WIDTH), blk(H_WIDTH)],
        out_shape=[jax.ShapeDtypeStruct((b, t, A_WIDTH), BF16), jax.ShapeDtypeStruct((b, t, M_WIDTH), BF16),
                   jax.ShapeDtypeStruct((b, t, H_WIDTH), BF16)],
        scratch_shapes=[pltpu.VMEM((t // Q_TILE, LANES, cols), BF16),
                        pltpu.VMEM((t // K_TILE, LANES + DEN_ROWS, K_TILE), BF16),
                        pltpu.VMEM((K_TILE, Q_TILE), jnp.int32),
                        pltpu.VMEM((K_TILE, cols), F32), pltpu.VMEM((K_TILE, cols), F32),
                        pltpu.VMEM((K_TILE, cols), BF16), pltpu.VMEM((K_TILE, cols), BF16),
                        pltpu.VMEM((1, cols), F32), pltpu.VMEM((1, cols), F32),
                        pltpu.VMEM((1, cols), F32),
                        pltpu.VMEM((LANES + DEN_ROWS, cols), F32),
                        pltpu.VMEM((8, 2 * M_WIDTH), F32),
                        pltpu.VMEM((2, LANES, 2 * LANES), F32),
                        pltpu.VMEM((8, LANES), F32),
                        pltpu.VMEM((2, LANES, ns * LANES), F32),
                        pltpu.VMEM((2, LANES, LANES), F32)],
        compiler_params=pltpu.CompilerParams(dimension_semantics=("arbitrary", "arbitrary"),
                                             vmem_limit_bytes=VMEM_LIMIT),
        name="mixers",
    )(tab, lam, qh, kh, vh, sg, um, gc, gt3, conv_w, mg, uh, uf, lb, hg)


def _rope_tables(t):
    half = ROT_DIM // 2
    inv = ROPE_THETA ** (-jnp.arange(half, dtype=F32) / half)
    ang = jnp.arange(t, dtype=F32)[:, None] * inv[None, :]
    cos, sin = jnp.cos(ang), jnp.sin(ang)
    pad = jnp.zeros((t, HEAD_DIM - ROT_DIM), F32)
    zero = jnp.zeros((t, half), F32)
    c = jnp.concatenate([cos, cos, pad + 1.0], axis=1)
    s1 = jnp.concatenate([-sin, zero, pad], axis=1)
    s2 = jnp.concatenate([zero, sin, pad], axis=1)
    tile = lambda a: jnp.concatenate([a, a], axis=1)
    return tile(c), tile(s1), tile(s2)


def kernel(x, norm1_g, w_in, mlstm_conv_w, mlstm_gate_b, mlstm_norm_g, diff_q_norm_g, diff_k_norm_g,
           diff_lambda, diff_subln_g, hgrn_lb_param, hgrn_norm_g, w_out, norm2_g, w_up, w_down):
    b, t, d = x.shape
    n = b * t
    assert d == D_MODEL and t % max(ROW_TILE, R_BLOCK, Q_TILE, K_TILE) == 0
    assert K_TILE % Q_TILE == 0

    sm = jax.nn.softmax(hgrn_lb_param.astype(F32), axis=0)
    lower_bounds = jnp.cumsum(sm, axis=0) - sm[0]
    cos, s1, s2 = _rope_tables(t)
    c_g = 4 * M_WIDTH
    c_a = c_g + 8
    c_h = c_a + 3 * A_WIDTH
    head_tile = lambda g, width: jnp.tile(g.astype(F32), width // HEAD_DIM).reshape(1, width)

    x2 = x.reshape(n, d)
    for l in range(DEPTH):
        lambda_init = 0.8 - 0.6 * math.exp(-0.3 * l)
        w = w_in[l]
        cols = lambda lo, hi: w[:, lo:hi].astype(BF16)
        wg = jnp.pad(w[:, c_g:c_a], ((0, 0), (0, LANES - 8))).astype(BF16)
        wgt = jnp.pad(w[:, c_g:c_a].T, ((0, 8), (0, 0))).astype(BF16)
        gate_b = mlstm_gate_b[l].astype(F32).reshape(1, 8)
        bg = jnp.pad(gate_b, ((0, 0), (0, LANES - 8)))
        bgt = jnp.broadcast_to(jnp.pad(gate_b.reshape(8, 1), ((0, 8), (0, 0))), (16, LANES))
        wh = jnp.concatenate([w[:, c_h:c_h + H_WIDTH], w[:, c_h + 2 * H_WIDTH:]], axis=1).astype(BF16)
        um, qh, kh, vh, uh, uf, gc, gt = _inproj(
            x2, norm1_g[l].reshape(1, d), cols(0, c_g),
            cols(c_a, c_a + A_WIDTH), cols(c_a + A_WIDTH, c_a + 2 * A_WIDTH), cols(c_a + 2 * A_WIDTH, c_h),
            wh, cols(c_h + H_WIDTH, c_h + 2 * H_WIDTH), wg, wgt, bg, bgt,
            head_tile(diff_q_norm_g[l], A_WIDTH), head_tile(diff_k_norm_g[l], A_WIDTH), cos, s1, s2)

        lv = diff_lambda[l].astype(F32)
        lam = jnp.exp(jnp.sum(lv[0] * lv[1])) - jnp.exp(jnp.sum(lv[2] * lv[3])) + lambda_init
        ya, ym, yh = _mixers(
            lam.reshape(1), qh.reshape(b, t, -1), kh.reshape(b, t, -1), vh.reshape(b, t, -1),
            diff_subln_g[l].astype(F32).reshape(1, LANES), 1.0 - lambda_init,
            um.reshape(b, t, -1), gc.reshape(b, t, LANES), gt, mlstm_conv_w[l].astype(F32),
            head_tile(mlstm_norm_g[l], LANES), uh.reshape(b, t, -1), uf.reshape(b, t, -1),
            lower_bounds[l].reshape(1, H_WIDTH), head_tile(hgrn_norm_g[l], H_WIDTH))

        x2 = _post(x2, ym.reshape(n, -1), ya.reshape(n, -1), yh.reshape(n, -1), w_out[l].astype(BF16),
                   norm2_g[l].reshape(1, d), w_up[l].astype(BF16), w_down[l].astype(BF16))
    return x2.reshape(b, t, d)
```

```python
import functools
import math

import jax
import jax.numpy as jnp
from jax import lax
from jax.experimental import pallas as pl
from jax.experimental.pallas import tpu as pltpu

F32 = jnp.float32
BF16 = jnp.bfloat16

D_MODEL = 1024
DEPTH = 2
D_FF = 4 * D_MODEL
EPS = 1e-6
NEG_BIG = -1e30
HEAD_DIM = 64
M_WIDTH = 256
M_CONV = 4
A_WIDTH = 512
A_HEADS = 4
ROT_DIM = 16
ROPE_THETA = 500000.0
H_WIDTH = 256
LANES = 128

VMEM_LIMIT = 56 * 1024 * 1024

ROW_TILE = 512
COL_CHUNK = 512
M_CHUNK = 128
R_BLOCK = 256
H_SUB = 16
DEN_ROWS = 16
BLOCKS_PER_HEAD = 2
Q_TILE = 256
K_TILE = 512


def _nt_dot(a, b):
    return lax.dot_general(a, b, (((1,), (1,)), ((), ())), preferred_element_type=F32)


def _dot(a, b):
    return jnp.dot(a, b, preferred_element_type=F32)


def _block_ones(n, group):
    r = lax.broadcasted_iota(jnp.int32, (n, n), 0) // group
    c = lax.broadcasted_iota(jnp.int32, (n, n), 1) // group
    return jnp.where(r == c, 1.0, 0.0).astype(BF16)


def _group_mean(xsq, ones_bd, group):
    return _dot(xsq.astype(BF16), ones_bd) * (1.0 / group)


def _log_sigmoid(x):
    return jnp.minimum(x, 0.0) - jnp.log1p(jnp.exp(-jnp.abs(x)))


def _round_robin(streams, turns):
    live = list(streams)
    while live:
        for g, n in zip(list(streams), turns):
            for _ in range(n):
                if g in live and next(g, StopIteration) is StopIteration:
                    live.remove(g)


def _inproj_kernel(x_ref, g_ref, wm_ref, wq_ref, wk_ref, wv_ref, wh_ref, wf_ref, wg_ref, wgt_ref, bg_ref, bgt_ref,
                   qg_ref, kg_ref, cos_ref, s1_ref, s2_ref,
                   um_ref, q_ref, k_ref, v_ref, uh_ref, uf_ref, gc_ref, gt_ref):
    x = x_ref[...]
    ms = jnp.mean(x * x, axis=-1, keepdims=True)
    h = (x * lax.rsqrt(ms + EPS) * g_ref[...]).astype(BF16)

    ones_bd2 = _block_ones(2 * LANES, HEAD_DIM)
    cos, s1, s2 = (jnp.concatenate([r[...], r[...]], axis=1) for r in (cos_ref, s1_ref, s2_ref))

    def project(w_ref, o_ref):
        width = w_ref.shape[1]
        for c0 in range(0, width, COL_CHUNK):
            sl = slice(c0, min(c0 + COL_CHUNK, width))
            o_ref[:, sl] = _dot(h, w_ref[:, sl]).astype(o_ref.dtype)
            yield

    def plain():
        yield from project(wm_ref, um_ref)
        yield from project(wv_ref, v_ref)
        yield from project(wh_ref, uh_ref)
        yield from project(wf_ref, uf_ref)
        gc_ref[...] = _dot(h, wg_ref[...]) + bg_ref[...]
        gt_ref[...] = _nt_dot(wgt_ref[...], h) + bgt_ref[:, 0:1]
        yield

    def rotated(w_ref, gain_ref, o_ref, scale):
        u = _dot(h, w_ref[...])
        yield
        for i in range(A_WIDTH // (2 * LANES)):
            sl = slice(2 * i * LANES, 2 * (i + 1) * LANES)
            us = u[:, sl]
            un = us * lax.rsqrt(_group_mean(us * us, ones_bd2, HEAD_DIM) + EPS) * gain_ref[:, sl]
            rot = (un * cos + pltpu.roll(un, 2 * LANES - ROT_DIM // 2, 1) * s1
                   + pltpu.roll(un, ROT_DIM // 2, 1) * s2)
            o_ref[:, sl] = (rot * scale).astype(BF16)
            yield

    def attention_inputs():
        yield from rotated(wq_ref, qg_ref, q_ref, HEAD_DIM ** -0.5 * math.log2(math.e))
        yield from rotated(wk_ref, kg_ref, k_ref, 1.0)

    _round_robin((attention_inputs(), plain()), (1, 1))


def _inproj(x2, g, wm, wq, wk, wv, wh, wf, wg, wgt, bg, bgt, qg, kg, cos, s1, s2):
    n = x2.shape[0]
    t_blocks = cos.shape[0] // ROW_TILE
    const = lambda a: pl.BlockSpec(a.shape, lambda i: (0, 0))
    rows = lambda width: pl.BlockSpec((ROW_TILE, width), lambda i: (i, 0))
    tab = pl.BlockSpec((ROW_TILE, LANES), lambda i: (i % t_blocks, 0))
    out = lambda width, dtype: jax.ShapeDtypeStruct((n, width), dtype)
    return pl.pallas_call(
        _inproj_kernel,
        grid=(n // ROW_TILE,),
        in_specs=[rows(D_MODEL)] + [const(a) for a in (g, wm, wq, wk, wv, wh, wf, wg, wgt, bg, bgt, qg, kg)]
                 + [tab, tab, tab],
        out_specs=[rows(wm.shape[1]), rows(A_WIDTH), rows(A_WIDTH), rows(A_WIDTH), rows(wh.shape[1]),
                   rows(wf.shape[1]), rows(LANES), pl.BlockSpec((16, ROW_TILE), lambda i: (0, i))],
        out_shape=[out(wm.shape[1], BF16), out(A_WIDTH, BF16), out(A_WIDTH, BF16), out(A_WIDTH, BF16),
                   out(wh.shape[1], BF16), out(wf.shape[1], F32), out(LANES, F32),
                   jax.ShapeDtypeStruct((16, n), F32)],
        compiler_params=pltpu.CompilerParams(dimension_semantics=("arbitrary",), vmem_limit_bytes=VMEM_LIMIT),
        name="inproj",
    )(x2, g, wm, wq, wk, wv, wh, wf, wg, wgt, bg, bgt, qg, kg, cos, s1, s2)


def _post_kernel(x_ref, ym_ref, ya_ref, yh_ref, wo_ref, g2_ref, wu_ref, wd_ref, o_ref):
    x1 = x_ref[...]
    x1 = x1 + _dot(ym_ref[...], wo_ref[0:M_WIDTH, :])
    x1 = x1 + _dot(ya_ref[...], wo_ref[M_WIDTH:M_WIDTH + A_WIDTH, :])
    x1 = x1 + _dot(yh_ref[...], wo_ref[M_WIDTH + A_WIDTH:, :])
    ms = jnp.mean(x1 * x1, axis=-1, keepdims=True)
    h2 = (x1 * lax.rsqrt(ms + EPS) * g2_ref[...]).astype(BF16)
    o_ref[...] = x1
    for c in range(D_FF // COL_CHUNK):
        sl = slice(c * COL_CHUNK, (c + 1) * COL_CHUNK)
        a = jnp.maximum(_dot(h2, wu_ref[:, sl]), 0.0)
        o_ref[...] += _dot((a * a).astype(BF16), wd_ref[sl, :])


def _post(x2, ym, ya, yh, wo, g2, wu, wd):
    n = x2.shape[0]
    const = lambda shape: pl.BlockSpec(shape, lambda i: (0, 0), pipeline_mode=pl.Buffered(1))
    rows = lambda width: pl.BlockSpec((ROW_TILE, width), lambda i: (i, 0))
    return pl.pallas_call(
        _post_kernel,
        grid=(n // ROW_TILE,),
        in_specs=[rows(D_MODEL), rows(M_WIDTH), rows(A_WIDTH), rows(H_WIDTH),
                  const(wo.shape), const((1, D_MODEL)), const(wu.shape), const(wd.shape)],
        out_specs=rows(D_MODEL),
        out_shape=jax.ShapeDtypeStruct((n, D_MODEL), F32),
        compiler_params=pltpu.CompilerParams(dimension_semantics=("arbitrary",), vmem_limit_bytes=VMEM_LIMIT),
        name="post",
    )(x2, ym, ya, yh, wo, g2, wu, wd)


def _mlstm_stream(um_ref, gc_ref, gt_ref, cw_ref, ng_ref, y_ref, tail_ref, gn_ref, m_ref):
    L = M_CHUNK
    R = R_BLOCK

    xin = um_ref[:, 0:2 * M_WIDTH].astype(F32)
    x3 = xin.reshape(R // 8, 8, 2 * M_WIDTH)
    tail = tail_ref[...].reshape(1, 8, 2 * M_WIDTH)
    sub = lax.broadcasted_iota(jnp.int32, x3.shape, 1)
    conv = None
    for j in range(M_CONV - 1):
        back = M_CONV - 1 - j
        rolled = pltpu.roll(x3, back, 1)
        prev = jnp.concatenate([pltpu.roll(tail, back, 1), rolled[:-1]], axis=0)
        term = jnp.where(sub >= back, rolled, prev) * cw_ref[j:j + 1, :]
        conv = term if conv is None else conv + term
    conv = (conv + x3 * cw_ref[M_CONV - 1:M_CONV, :]).reshape(R, 2 * M_WIDTH)
    tail_ref[...] = xin[R - 8:R, :]
    qk = conv * jax.nn.sigmoid(conv)

    gcol_all = gc_ref[...]
    grow_all = gt_ref[...]
    logf_col_all = _log_sigmoid(gcol_all)
    logf_row_all = _log_sigmoid(grow_all)

    ti = lax.broadcasted_iota(jnp.int32, (L, L), 0)
    si = lax.broadcasted_iota(jnp.int32, (L, L), 1)
    causal = si <= ti
    lane = lax.broadcasted_iota(jnp.int32, (L, LANES), 1)
    left = lane < HEAD_DIM
    ones_bd = _block_ones(LANES, HEAD_DIM)
    gr = lax.broadcasted_iota(jnp.int32, (LANES, 2 * LANES), 0)
    gc_i = lax.broadcasted_iota(jnp.int32, (LANES, 2 * LANES), 1)
    g_rowleft = gr < HEAD_DIM
    g_block = g_rowleft == ((gc_i % LANES) < HEAD_DIM)
    yield

    for ci in range(R // L):
        rows = slice(ci * L, (ci + 1) * L)
        gcol, grow = gcol_all[rows], grow_all[:, rows]
        logf_col, logf_row = logf_col_all[rows], logf_row_all[:, rows]
        for p in range(2):
            sl = slice(p * LANES, (p + 1) * LANES)
            q_pair = qk[rows, p * LANES:(p + 1) * LANES]
            k_pair = qk[rows, M_WIDTH + p * LANES:M_WIDTH + (p + 1) * LANES] * (HEAD_DIM ** -0.5)
            v_pair = um_ref[rows, 2 * M_WIDTH + p * LANES:2 * M_WIDTH + (p + 1) * LANES]
            o_pair = um_ref[rows, 3 * M_WIDTH + p * LANES:3 * M_WIDTH + (p + 1) * LANES].astype(F32)
            v1 = jnp.concatenate([v_pair, jnp.ones((L, LANES), v_pair.dtype)], axis=1)
            k_bf = k_pair.astype(BF16)
            intra, scale, einv, wcol, decay = [], [], [], [], []
            for hh in range(2):
                h = 2 * p + hh
                m_prev = m_ref[h:h + 1, 0:1]
                i_row = grow[h:h + 1, :]
                i_col = gcol[:, h:h + 1]
                lf_row = logf_row[4 + h:5 + h, :]
                lf_col = logf_col[:, 4 + h:5 + h]
                b_col = jnp.sum(jnp.where(causal, lf_row, 0.0), axis=1, keepdims=True)
                b_row = jnp.sum(jnp.where(ti <= si, lf_col, 0.0), axis=0, keepdims=True)
                a_row = i_row - b_row
                a_col = i_col - b_col
                dm = jnp.where(causal, a_row, NEG_BIG)
                m_col = jnp.maximum(jnp.max(dm, axis=1, keepdims=True), m_prev)
                e = jnp.exp(dm - m_col)
                qm = jnp.where(left if hh == 0 else jnp.logical_not(left), q_pair, 0.0).astype(BF16)
                s = _nt_dot(qm, k_bf) * e
                intra.append(_dot(s.astype(BF16), v1))
                m_last = m_col[L - 1:L, :]
                scale.append(jnp.exp(m_prev - m_col))
                einv.append(jnp.exp(-(b_col + m_col)))
                wcol.append(jnp.exp(a_col - m_last))
                decay.append(jnp.exp(m_prev - m_last))
                m_ref[h:h + 1, :] = jnp.broadcast_to(b_col[L - 1:L, :] + m_last, (1, LANES))

            gn = gn_ref[p]
            inter = _dot(q_pair.astype(BF16), gn.astype(BF16))
            sc = jnp.where(left, scale[0], scale[1])
            num = jnp.where(left, intra[0][:, :LANES], intra[1][:, :LANES]) + sc * inter[:, :LANES]
            den = jnp.where(left, intra[0][:, LANES:], intra[1][:, LANES:]) + sc * inter[:, LANES:]
            hv = num / jnp.maximum(jnp.abs(den), jnp.where(left, einv[0], einv[1]))

            kw = k_pair * jnp.where(left, wcol[0], wcol[1])
            upd = _dot(kw.T.astype(BF16), v1)
            gn_ref[p] = jnp.where(g_rowleft, decay[0], decay[1]) * gn + jnp.where(g_block, upd, 0.0)

            ms = _group_mean(hv * hv, ones_bd, HEAD_DIM)
            y = jax.nn.sigmoid(o_pair) * (hv * lax.rsqrt(ms + EPS) * ng_ref[...])
            y_ref[rows, sl] = y.astype(y_ref.dtype)
            yield


def _hgrn_stream(uh_ref, uf_ref, lb_ref, ng_ref, y_ref, w_ref, st_ref):
    R = R_BLOCK
    NS = R // H_SUB

    lb = lb_ref[...]
    fp = uf_ref[...]
    hq = uh_ref[:, 0:H_WIDTH].astype(F32)
    q = hq * jax.nn.sigmoid(hq)
    logf = jnp.log(lb + (1.0 - lb) * jax.nn.sigmoid(fp)) * math.log2(math.e)
    k = (1.0 - lb) * jax.nn.sigmoid(-fp)
    v = uh_ref[:, H_WIDTH:2 * H_WIDTH].astype(F32)

    G = R // H_SUB
    HALF = H_SUB // 2

    def halves(x):
        x4 = x.reshape(G, 2, HALF, H_WIDTH)
        return x4[:, 0], x4[:, 1]

    def whole(lo, hi):
        return jnp.stack([lo, hi], axis=1).reshape(R, H_WIDTH)

    sub = lax.broadcasted_iota(jnp.int32, (G, HALF, H_WIDTH), 1)

    def cumsum_half(x):
        shift = 1
        while shift < HALF:
            x = x + jnp.where(sub >= shift, pltpu.roll(x, shift, 1), 0.0)
            shift *= 2
        return x

    lf_lo, lf_hi = halves(logf)
    bc_lo = cumsum_half(lf_lo)
    bc_hi = cumsum_half(lf_hi) + bc_lo[:, HALF - 1:HALF, :]
    bc = whole(bc_lo, bc_hi)
    q_lo, q_hi = halves(q)
    k_lo, k_hi = halves(k)
    v_lo, v_hi = halves(v)
    ones_bd2 = _block_ones(H_WIDTH, HEAD_DIM)

    bl_rows = jnp.concatenate(
        [jnp.broadcast_to(bc[(j + 1) * H_SUB - 1:(j + 1) * H_SUB, :], (H_SUB, H_WIDTH)) for j in range(NS)], axis=0)
    qe = q * jnp.exp2(bc)
    ke = k * jnp.exp2(bl_rows - bc)
    tcol = lax.broadcasted_iota(jnp.int32, (LANES, R), 1) // H_SUB
    trow = lax.broadcasted_iota(jnp.int32, (R, LANES), 0) // H_SUB
    sr = lax.broadcasted_iota(jnp.int32, (LANES, LANES), 0) < HEAD_DIM
    s_block = sr == (lax.broadcasted_iota(jnp.int32, (LANES, LANES), 1) < HEAD_DIM)
    yield

    outs = []

    def history(p):
        sl = slice(p * LANES, (p + 1) * LANES)
        v_t = v[:, sl].T
        v_stack = jnp.concatenate([jnp.where(tcol == j, v_t, 0.0) for j in range(NS)], axis=0).astype(BF16)
        upd = _dot(v_stack, ke[:, sl].astype(BF16))
        yield
        st = st_ref[p]
        for j in range(NS):
            w_ref[p, :, j * LANES:(j + 1) * LANES] = st
            a_row = jnp.exp2(bc[(j + 1) * H_SUB - 1:(j + 1) * H_SUB, sl])
            st = st * a_row + jnp.where(s_block, upd[j * LANES:(j + 1) * LANES, :], 0.0)
        st_ref[p] = st
        q_exp = jnp.concatenate([jnp.where(trow == j, qe[:, sl], 0.0) for j in range(NS)], axis=1).astype(BF16)
        outs.append(_nt_dot(q_exp, w_ref[p].astype(BF16)))
        yield

    def weights(qq, bq, bs, ks, valid):
        dlt = bq - bs
        if valid is not None:
            dlt = jnp.where(valid, dlt, NEG_BIG)
        return (qq * ks * jnp.exp2(dlt)).reshape(G * HALF, H_WIDTH)

    acc = [jnp.zeros((G, HALF, H_WIDTH), F32), jnp.zeros((G, HALF, H_WIDTH), F32)]

    def direct():
        for b in range(HALF):
            if b == 0:
                valid = None
                bs_l, ks_l, vs_l = bc_lo, k_lo, v_lo
                bs_h, ks_h, vs_h = bc_hi, k_hi, v_hi
            else:
                valid = sub >= b
                bs_l, ks_l, vs_l = (pltpu.roll(x, b, 1) for x in (bc_lo, k_lo, v_lo))
                bs_h, ks_h, vs_h = (jnp.where(valid, pltpu.roll(x, b, 1), y)
                                    for x, y in ((bc_hi, bs_l), (k_hi, ks_l), (v_hi, vs_l)))
            pr = jnp.concatenate([weights(q_lo, bc_lo, bs_l, ks_l, valid),
                                  weights(q_hi, bc_hi, bs_h, ks_h, None),
                                  weights(q_hi, bc_hi, bs_l, ks_l, valid)], axis=0).astype(BF16)
            red = _dot(pr, ones_bd2)
            red = red.reshape(3, G, HALF, H_WIDTH)
            acc[0] = acc[0] + red[0] * vs_l
            acc[1] = acc[1] + red[1] * vs_h + red[2] * vs_l
            yield

    hist = [history(0), history(1)]
    dire = direct()
    for piece in range(HALF):
        if piece % 2 == 0:
            next(hist[piece // 4])
            yield
        next(dire)
        yield
    o = whole(acc[0], acc[1]) + jnp.concatenate(outs, axis=1)

    ms = _group_mean(o * o, ones_bd2, HEAD_DIM)
    hg = uh_ref[:, 2 * H_WIDTH:3 * H_WIDTH].astype(F32)
    y = (o * lax.rsqrt(ms + EPS) * ng_ref[...]) * (hg * jax.nn.sigmoid(hg))
    y_ref[...] = y.astype(y_ref.dtype)
    yield


def _mixers_kernel(lam_ref, q_ref, k_ref, v_ref, sg_ref,
                   um_ref, gc_ref, gt_ref, cw_ref, mg_ref, uh_ref, uf_ref, lb_ref, hg_ref,
                   ya_ref, ym_ref, yh_ref,
                   qt_ref, vt_ref, d_ref, s0_ref, s1_ref, p0_ref, p1_ref, a0_ref, a1_ref, m_ref, acc_ref,
                   tail_ref, gn_ref, mm_ref, w_ref, st_ref, *, out_scale, steps):
    @pl.when(pl.program_id(1) == 0)
    def _():
        tail_ref[...] = jnp.zeros(tail_ref.shape, F32)
        gn_ref[...] = jnp.zeros(gn_ref.shape, F32)
        mm_ref[...] = jnp.zeros(mm_ref.shape, F32)
        st_ref[...] = jnp.zeros(st_ref.shape, F32)

    n_q = q_ref.shape[1] // Q_TILE
    lane = lax.broadcasted_iota(jnp.int32, (Q_TILE, LANES), 1)
    for i in range(n_q):
        q = q_ref[0, i * Q_TILE:(i + 1) * Q_TILE, :].astype(F32)
        qt_ref[i, :, 0:Q_TILE] = jnp.where(lane < HEAD_DIM, q, 0.0).T.astype(BF16)
        qt_ref[i, :, Q_TILE:] = jnp.where(lane >= HEAD_DIM, q, 0.0).T.astype(BF16)
    for j in range(q_ref.shape[1] // K_TILE):
        ks = slice(j * K_TILE, (j + 1) * K_TILE)
        vt_ref[0:LANES, ks] = v_ref[0, ks, :].astype(F32).T.astype(BF16)
    vt_ref[LANES:, :] = jnp.ones((DEN_ROWS, vt_ref.shape[1]), BF16)
    d_ref[...] = (lax.broadcasted_iota(jnp.int32, (K_TILE, Q_TILE), 1)
                  - lax.broadcasted_iota(jnp.int32, (K_TILE, Q_TILE), 0))

    halves = (slice(0, Q_TILE), slice(Q_TILE, 2 * Q_TILE))

    def scores(step, s_ref, cs):
        qi, k0, kw, _ = step
        s_ref[0:kw, cs] = _dot(k_ref[0, k0:k0 + kw, :], qt_ref[qi, :, cs])

    def numerators(step, s_ref, p_ref, a_ref, cs):
        qi, k0, kw, diag = step
        s = s_ref[0:kw, cs]
        if diag:
            s = jnp.where(d_ref[0:kw, :] >= k0 - qi * Q_TILE, s, NEG_BIG)
        m_old = m_ref[:, cs]
        m_new = jnp.maximum(m_old, jnp.max(s, axis=0, keepdims=True))
        p_ref[0:kw, cs] = jnp.exp2(s - m_new).astype(BF16)
        a_ref[:, cs] = jnp.exp2(m_old - m_new)
        m_ref[:, cs] = jnp.full(m_new.shape, NEG_BIG, F32) if diag else m_new

    def values(step, p_ref, a_ref, cs):
        _, k0, kw, _ = step
        acc = a_ref[:, cs] * acc_ref[:, cs] + _dot(vt_ref[:, k0:k0 + kw], p_ref[0:kw, cs])
        acc_ref[:, cs] = acc
        return acc

    def finish(step, acc0, acc1):
        qi = step[0]
        on0 = acc0[0:LANES, :] / acc0[LANES:LANES + 1, :]
        on1 = acc1[0:LANES, :] / acc1[LANES:LANES + 1, :]
        o_t = on0 - lam_ref[0] * on1
        ms = jnp.mean(o_t * o_t, axis=0, keepdims=True)
        y = ((o_t * lax.rsqrt(ms + EPS)).T * sg_ref[...]) * out_scale
        ya_ref[0, qi * Q_TILE:(qi + 1) * Q_TILE, :] = y.astype(ya_ref.dtype)

    m_ref[...] = jnp.full(m_ref.shape, NEG_BIG, F32)
    acc_ref[...] = jnp.zeros(acc_ref.shape, F32)
    n_steps = len(steps)
    per_block = n_steps // BLOCKS_PER_HEAD
    bufs = ((s0_ref, p0_ref, a0_ref), (s1_ref, p1_ref, a1_ref))
    for cs in halves:
        scores(steps[0], s0_ref, cs)
        scores(steps[1], s1_ref, cs)
        numerators(steps[0], s0_ref, p0_ref, a0_ref, cs)

    def attention_stream(i):
        for t in range(per_block * i, per_block * (i + 1)):
            cur, nxt = bufs[t % 2], bufs[(t + 1) % 2]
            accs = []
            for cs in halves:
                if t + 2 < n_steps:
                    scores(steps[t + 2], cur[0], cs)
                if t + 1 < n_steps:
                    numerators(steps[t + 1], *nxt, cs)
                accs.append(values(steps[t], cur[1], cur[2], cs))
                yield
            if steps[t][3]:
                finish(steps[t], *accs)

    for i in range(BLOCKS_PER_HEAD):
        rows = slice(i * R_BLOCK, (i + 1) * R_BLOCK)
        mlstm = _mlstm_stream(um_ref.at[0, rows], gc_ref.at[0, rows], gt_ref.at[i], cw_ref, mg_ref,
                              ym_ref.at[0, rows], tail_ref, gn_ref, mm_ref)
        hgrn = _hgrn_stream(uh_ref.at[0, rows], uf_ref.at[0, rows], lb_ref, hg_ref, yh_ref.at[0, rows],
                            w_ref, st_ref)
        _round_robin((attention_stream(i), hgrn, mlstm), (4, 3, 1))


def _attn_schedule(t):
    n_q = t // Q_TILE
    groups = []
    for g in range(BLOCKS_PER_HEAD):
        steps = []
        for qi in range(g, n_q, BLOCKS_PER_HEAD):
            n_past = (qi * Q_TILE) // K_TILE
            steps += [(qi, kj * K_TILE, K_TILE, False) for kj in range(n_past)]
            steps.append((qi, n_past * K_TILE, (qi + 1) * Q_TILE - n_past * K_TILE, True))
        groups.append(steps)
    assert all(len(g) == len(groups[0]) for g in groups)
    return tuple(sum(groups, []))


def _mixers(lam, qh, kh, vh, sg, out_scale, um, gc, gt, conv_w, mg, uh, uf, lb, hg):
    b, t, _ = qh.shape
    assert t == A_HEADS * BLOCKS_PER_HEAD * R_BLOCK
    steps = _attn_schedule(t)
    rows_h = BLOCKS_PER_HEAD * R_BLOCK
    nb = t // R_BLOCK
    gt3 = gt.reshape(16, b * nb, R_BLOCK).transpose(1, 0, 2)
    seq = pl.BlockSpec((1, t, LANES), lambda i, h: (i, 0, h))
    smem = pl.BlockSpec(memory_space=pltpu.SMEM)
    blk = lambda width: pl.BlockSpec((1, rows_h, width), lambda i, h: (i, h, 0))
    const = lambda a: pl.BlockSpec(a.shape, lambda i, h: (0, 0))
    cols = 2 * Q_TILE
    ns = R_BLOCK // H_SUB
    return pl.pallas_call(
        functools.partial(_mixers_kernel, out_scale=out_scale, steps=steps),
        grid=(b, A_HEADS),
        in_specs=[smem, seq, seq, seq, const(sg),
                  blk(4 * M_WIDTH), blk(LANES),
                  pl.BlockSpec((BLOCKS_PER_HEAD, 16, R_BLOCK), lambda i, h: (i * A_HEADS + h, 0, 0)),
                  const(conv_w), const(mg), blk(3 * H_WIDTH), blk(H_WIDTH), const(lb), const(hg)],
        out_specs=[seq, blk(M_WIDTH), blk(H_WIDTH)],
        out_shape=[jax.ShapeDtypeStruct((b, t, A_WIDTH), BF16), jax.ShapeDtypeStruct((b, t, M_WIDTH), BF16),
                   jax.ShapeDtypeStruct((b, t, H_WIDTH), BF16)],
        scratch_shapes=[pltpu.VMEM((t // Q_TILE, LANES, cols), BF16),
                        pltpu.VMEM((LANES + DEN_ROWS, t), BF16),
                        pltpu.VMEM((K_TILE, Q_TILE), jnp.int32),
                        pltpu.VMEM((K_TILE, cols), F32), pltpu.VMEM((K_TILE, cols), F32),
                        pltpu.VMEM((K_TILE, cols), BF16), pltpu.VMEM((K_TILE, cols), BF16),
                        pltpu.VMEM((1, cols), F32), pltpu.VMEM((1, cols), F32),
                        pltpu.VMEM((1, cols), F32),
                        pltpu.VMEM((LANES + DEN_ROWS, cols), F32),
                        pltpu.VMEM((8, 2 * M_WIDTH), F32),
                        pltpu.VMEM((2, LANES, 2 * LANES), F32),
                        pltpu.VMEM((8, LANES), F32),
                        pltpu.VMEM((2, LANES, ns * LANES), F32),
                        pltpu.VMEM((2, LANES, LANES), F32)],
        compiler_params=pltpu.CompilerParams(dimension_semantics=("arbitrary", "arbitrary"),
                                             vmem_limit_bytes=VMEM_LIMIT),
        name="mixers",
    )(lam, qh, kh, vh, sg, um, gc, gt3, conv_w, mg, uh, uf, lb, hg)


def _rope_tables(t):
    half = ROT_DIM // 2
    inv = ROPE_THETA ** (-jnp.arange(half, dtype=F32) / half)
    ang = jnp.arange(t, dtype=F32)[:, None] * inv[None, :]
    cos, sin = jnp.cos(ang), jnp.sin(ang)
    pad = jnp.zeros((t, HEAD_DIM - ROT_DIM), F32)
    zero = jnp.zeros((t, half), F32)
    c = jnp.concatenate([cos, cos, pad + 1.0], axis=1)
    s1 = jnp.concatenate([-sin, zero, pad], axis=1)
    s2 = jnp.concatenate([zero, sin, pad], axis=1)
    tile = lambda a: jnp.concatenate([a, a], axis=1)
    return tile(c), tile(s1), tile(s2)


def kernel(x, norm1_g, w_in, mlstm_conv_w, mlstm_gate_b, mlstm_norm_g, diff_q_norm_g, diff_k_norm_g,
           diff_lambda, diff_subln_g, hgrn_lb_param, hgrn_norm_g, w_out, norm2_g, w_up, w_down):
    b, t, d = x.shape
    n = b * t
    assert d == D_MODEL and t % max(ROW_TILE, R_BLOCK, Q_TILE, K_TILE) == 0
    assert K_TILE % Q_TILE == 0

    sm = jax.nn.softmax(hgrn_lb_param.astype(F32), axis=0)
    lower_bounds = jnp.cumsum(sm, axis=0) - sm[0]
    cos, s1, s2 = _rope_tables(t)
    c_g = 4 * M_WIDTH
    c_a = c_g + 8
    c_h = c_a + 3 * A_WIDTH
    head_tile = lambda g, width: jnp.tile(g.astype(F32), width // HEAD_DIM).reshape(1, width)

    x2 = x.reshape(n, d)
    for l in range(DEPTH):
        lambda_init = 0.8 - 0.6 * math.exp(-0.3 * l)
        w = w_in[l]
        cols = lambda lo, hi: w[:, lo:hi].astype(BF16)
        wg = jnp.pad(w[:, c_g:c_a], ((0, 0), (0, LANES - 8))).astype(BF16)
        wgt = jnp.pad(w[:, c_g:c_a].T, ((0, 8), (0, 0))).astype(BF16)
        gate_b = mlstm_gate_b[l].astype(F32).reshape(1, 8)
        bg = jnp.pad(gate_b, ((0, 0), (0, LANES - 8)))
        bgt = jnp.broadcast_to(jnp.pad(gate_b.reshape(8, 1), ((0, 8), (0, 0))), (16, LANES))
        wh = jnp.concatenate([w[:, c_h:c_h + H_WIDTH], w[:, c_h + 2 * H_WIDTH:]], axis=1).astype(BF16)
        um, qh, kh, vh, uh, uf, gc, gt = _inproj(
            x2, norm1_g[l].reshape(1, d), cols(0, c_g),
            cols(c_a, c_a + A_WIDTH), cols(c_a + A_WIDTH, c_a + 2 * A_WIDTH), cols(c_a + 2 * A_WIDTH, c_h),
            wh, cols(c_h + H_WIDTH, c_h + 2 * H_WIDTH), wg, wgt, bg, bgt,
            head_tile(diff_q_norm_g[l], A_WIDTH), head_tile(diff_k_norm_g[l], A_WIDTH), cos, s1, s2)

        lv = diff_lambda[l].astype(F32)
        lam = jnp.exp(jnp.sum(lv[0] * lv[1])) - jnp.exp(jnp.sum(lv[2] * lv[3])) + lambda_init
        ya, ym, yh = _mixers(
            lam.reshape(1), qh.reshape(b, t, -1), kh.reshape(b, t, -1), vh.reshape(b, t, -1),
            diff_subln_g[l].astype(F32).reshape(1, LANES), 1.0 - lambda_init,
            um.reshape(b, t, -1), gc.reshape(b, t, LANES), gt, mlstm_conv_w[l].astype(F32),
            head_tile(mlstm_norm_g[l], LANES), uh.reshape(b, t, -1), uf.reshape(b, t, -1),
            lower_bounds[l].reshape(1, H_WIDTH), head_tile(hgrn_norm_g[l], H_WIDTH))

        x2 = _post(x2, ym.reshape(n, -1), ya.reshape(n, -1), yh.reshape(n, -1), w_out[l].astype(BF16),
                   norm2_g[l].reshape(1, d), w_up[l].astype(BF16), w_down[l].astype(BF16))
    return x2.reshape(b, t, d)
```
